```python
import math
import jax, jax.numpy as jnp
from jax import lax
import numpy as np

D_MODEL = 1024
BATCH = 8
SEQ = 4096
DEPTH = 1

EPS = 1e-6
GDN_HEADS = 4
GDN_DK = 128
GDN_DV = 128
CONV_K = 4
CHUNK = 64
SWA_HEADS = 8
SWA_DH = 64
DILATED_BRANCHES = ((128, 1), (512, 4), (2048, 16))
SWA_BLOCK = 128
REL_BUCKETS = 32
REL_MAX_DIST = 2048
N_EXPERTS = 32
TOP_K = 4
D_FF = D_MODEL
SWIGLU_LIMIT = 7.0
SWIGLU_ALPHA = 1.702
MOE_BLOCK = 256
IN_WIDTHS = (GDN_HEADS * GDN_DK, GDN_HEADS * GDN_DK, GDN_HEADS * GDN_DV, GDN_HEADS * GDN_DV,
             GDN_HEADS, GDN_HEADS, SWA_HEADS * SWA_DH, SWA_HEADS * SWA_DH, SWA_HEADS * SWA_DH)
IN_DIM = sum(IN_WIDTHS)
CONV_DIM = 2 * GDN_HEADS * GDN_DK + GDN_HEADS * GDN_DV
MIX_WIDTH = GDN_HEADS * GDN_DV + SWA_HEADS * SWA_DH

kernel_name = 'hymba_gdn_dilated_swa_moe'


def _rmsnorm(x, w):
    xf = x.astype(jnp.float32)
    y = xf * lax.rsqrt(jnp.mean(xf * xf, axis=-1, keepdims=True) + EPS) * w.astype(jnp.float32)
    return y.astype(x.dtype)


def _l2norm(x):
    return x * lax.rsqrt(jnp.sum(x * x, axis=-1, keepdims=True) + EPS)


def _split_columns(z):
    offsets, acc = [], 0
    for w in IN_WIDTHS[:-1]:
        acc += w
        offsets.append(acc)
    return jnp.split(z, offsets, axis=-1)


def _causal_dwconv(x, w):
    c = x.shape[-1]
    return lax.conv_general_dilated(x, w.astype(x.dtype)[:, None, :], window_strides=(1,),
                                    padding=[(CONV_K - 1, 0)],
                                    dimension_numbers=('NWC', 'WIO', 'NWC'),
                                    feature_group_count=c)


def _gated_deltanet(q, k, v, beta, g):
    b, s, h, dk = q.shape
    dv = v.shape[-1]
    n = s // CHUNK

    def chunks(t):
        return t.reshape(b, n, CHUNK, h, -1).transpose(0, 3, 1, 2, 4)

    q, k, v = chunks(q), chunks(k), chunks(v)
    beta = beta.reshape(b, n, CHUNK, h).transpose(0, 3, 1, 2)
    g = g.reshape(b, n, CHUNK, h).transpose(0, 3, 1, 2)
    gc = jnp.cumsum(g, axis=-1)
    ii = np.arange(CHUNK)[:, None]
    jj = np.arange(CHUNK)[None, :]
    decay = jnp.exp(jnp.where(ii >= jj, gc[..., :, None] - gc[..., None, :], -jnp.inf))
    kb = k * beta[..., None]
    a_kk = jnp.where(ii > jj, jnp.einsum('bhnik,bhnjk->bhnij', kb, k) * decay, 0.0)
    m = a_kk + jnp.eye(CHUNK, dtype=a_kk.dtype)
    rhs = jnp.concatenate([v * beta[..., None], kb * jnp.exp(gc)[..., None]], axis=-1)
    sol = lax.linalg.triangular_solve(m, rhs, left_side=True, lower=True, unit_diagonal=True)
    u, w = sol[..., :dv], sol[..., dv:]
    a_qk = jnp.einsum('bhnik,bhnjk->bhnij', q, k) * decay
    q_dec = q * jnp.exp(gc)[..., None]
    k_dec = k * jnp.exp(gc[..., -1:] - gc)[..., None]
    chunk_decay = jnp.exp(gc[..., -1])

    def step(state, xs):
        u_n, w_n, qd_n, kd_n, a_n, cd_n = xs
        v_new = u_n - jnp.einsum('bhck,bhkv->bhcv', w_n, state)
        o_n = jnp.einsum('bhck,bhkv->bhcv', qd_n, state) + jnp.einsum('bhij,bhjv->bhiv', a_n, v_new)
        state = state * cd_n[..., None, None] + jnp.einsum('bhck,bhcv->bhkv', kd_n, v_new)
        return state, o_n

    xs = tuple(jnp.moveaxis(t, 2, 0) for t in (u, w, q_dec, k_dec, a_qk, chunk_decay))
    state0 = jnp.zeros((b, h, dk, dv), jnp.float32)
    _, o = lax.scan(step, state0, xs)
    return o.transpose(1, 0, 3, 2, 4).reshape(b, s, h, dv)


def _t5_bucket(dist):
    max_exact = REL_BUCKETS // 2
    n = np.maximum(dist, 0)
    large = max_exact + (np.log(np.maximum(n, 1) / max_exact) / math.log(REL_MAX_DIST / max_exact)
                         * (REL_BUCKETS - max_exact)).astype(np.int32)
    large = np.minimum(large, REL_BUCKETS - 1)
    return np.where(n < max_exact, n, large).astype(np.int32)


def _dilated_branch(q, k, v, rel_bias, window, dilation):
    b, s, h, dh = q.shape
    length = s // dilation
    nb = -(-length // SWA_BLOCK)
    lp = nb * SWA_BLOCK

    def split(t):
        t = t.reshape(b, length, dilation, h, dh).transpose(0, 2, 1, 3, 4).reshape(b * dilation, length, h, dh)
        return jnp.pad(t, ((0, 0), (0, lp - length), (0, 0), (0, 0)))

    def band(t):
        tb = t.reshape(-1, nb, SWA_BLOCK, h, dh)
        prev = jnp.pad(tb, ((0, 0), (1, 0), (0, 0), (0, 0), (0, 0)))[:, :-1]
        return jnp.concatenate([prev, tb], axis=2)

    qb = split(q).reshape(-1, nb, SWA_BLOCK, h, dh)
    kb, vb = band(split(k)), band(split(v))
    qi = np.arange(SWA_BLOCK)[:, None]
    kj = np.arange(2 * SWA_BLOCK)[None, :]
    steps = qi + SWA_BLOCK - kj
    valid_local = (steps >= 0) & (steps <= window // dilation)
    valid = np.where((np.arange(nb) == 0)[:, None, None], valid_local & (kj >= SWA_BLOCK), valid_local)
    bias = rel_bias.astype(jnp.float32)[jnp.asarray(_t5_bucket(steps * dilation))].transpose(2, 0, 1)
    scores = jnp.einsum('bnqhe,bnkhe->bhnqk', qb, kb) + bias[:, None]
    scores = jnp.where(valid[None, None], scores, -jnp.inf)
    mx = jnp.max(scores, axis=-1, keepdims=True)
    p = jnp.exp(scores - mx)
    den = jnp.sum(p, axis=-1)
    o = jnp.einsum('bhnqk,bnkhe->bnqhe', p, vb) / den.transpose(0, 2, 3, 1)[..., None]
    lse = (mx[..., 0] + jnp.log(den)).transpose(0, 2, 3, 1)
    o = o.reshape(b, dilation, lp, h, dh)[:, :, :length].transpose(0, 2, 1, 3, 4).reshape(b, s, h, dh)
    lse = lse.reshape(b, dilation, lp, h)[:, :, :length].transpose(0, 2, 1, 3).reshape(b, s, h)
    return o, lse


def _dilated_attention(q, k, v, rel_bias):
    outs, lses = [], []
    for window, dilation in DILATED_BRANCHES:
        o, lse = _dilated_branch(q, k, v, rel_bias, window, dilation)
        outs.append(o)
        lses.append(lse)
    wts = jax.nn.softmax(jnp.stack(lses), axis=0)
    return jnp.sum(wts[..., None] * jnp.stack(outs), axis=0)


def _moe(hn, w_router, b_router, w_gu, b_gu, w_dn, b_dn):
    b, s, d = hn.shape
    t = b * s
    hf = hn.reshape(t, d)
    logits = hf.astype(jnp.float32) @ w_router.astype(jnp.float32) + b_router.astype(jnp.float32)
    top_v, top_e = lax.top_k(logits, TOP_K)
    gates = jax.nn.softmax(top_v, axis=-1)
    e_flat = top_e.reshape(-1)
    tok_flat = jnp.repeat(jnp.arange(t, dtype=jnp.int32), TOP_K)
    g_flat = gates.reshape(-1)
    order = jnp.argsort(e_flat)
    e_s, tok_s, g_s = e_flat[order], tok_flat[order], g_flat[order]
    counts = jnp.bincount(e_flat, length=N_EXPERTS)
    padded = (counts + MOE_BLOCK - 1) // MOE_BLOCK * MOE_BLOCK
    raw_start = jnp.cumsum(counts) - counts
    pad_end = jnp.cumsum(padded)
    pad_start = pad_end - padded
    dest = pad_start[e_s] + jnp.arange(t * TOP_K, dtype=jnp.int32) - raw_start[e_s]
    p_rows = t * TOP_K + N_EXPERTS * MOE_BLOCK
    nblk = p_rows // MOE_BLOCK
    buf_tok = jnp.full((p_rows,), t, jnp.int32).at[dest].set(tok_s)
    buf_gate = jnp.zeros((p_rows,), jnp.float32).at[dest].set(g_s)
    blk_start = jnp.arange(nblk, dtype=jnp.int32) * MOE_BLOCK
    blk_expert = jnp.minimum(jnp.sum(blk_start[:, None] >= pad_end[None, :], axis=1), N_EXPERTS - 1)
    h_pad = jnp.concatenate([hf, jnp.zeros((1, d), hf.dtype)], axis=0)
    xb = h_pad[buf_tok].reshape(nblk, MOE_BLOCK, d)

    def expert_block(args):
        xblk, e = args
        gu = xblk @ w_gu[e] + b_gu[e]
        gate = jnp.minimum(gu[:, :D_FF], SWIGLU_LIMIT)
        up = jnp.clip(gu[:, D_FF:], -SWIGLU_LIMIT, SWIGLU_LIMIT)
        act = (up + 1.0) * gate * jax.nn.sigmoid(SWIGLU_ALPHA * gate)
        return act @ w_dn[e] + b_dn[e]

    yb = lax.map(expert_block, (xb, blk_expert)).reshape(p_rows, d)
    out = jnp.zeros((t + 1, d), jnp.float32).at[buf_tok].add(yb.astype(jnp.float32) * buf_gate[:, None])
    return out[:t].reshape(b, s, d).astype(hn.dtype)


def _layer(x, norm1_w, w_in, conv_w, a_log, dt_bias, gdn_norm_w, q_norm_w, k_norm_w, rel_bias,
           w_out, norm2_w, w_router, b_router, w_gate_up, b_gate_up, w_down, b_down):
    b, s, _ = x.shape
    f32 = jnp.float32
    z = _rmsnorm(x, norm1_w) @ w_in
    qa, ka, va, gate_a, beta_l, alpha_l, qb, kb, vb = _split_columns(z)
    qkv = jax.nn.silu(_causal_dwconv(jnp.concatenate([qa, ka, va], axis=-1), conv_w)).astype(f32)
    qa, ka, va = jnp.split(qkv, [GDN_HEADS * GDN_DK, 2 * GDN_HEADS * GDN_DK], axis=-1)
    qa = _l2norm(qa.reshape(b, s, GDN_HEADS, GDN_DK)) * (GDN_DK ** -0.5)
    ka = _l2norm(ka.reshape(b, s, GDN_HEADS, GDN_DK))
    va = va.reshape(b, s, GDN_HEADS, GDN_DV)
    beta = jax.nn.sigmoid(beta_l.astype(f32))
    g = -jnp.exp(a_log.astype(f32)) * jax.nn.softplus(alpha_l.astype(f32) + dt_bias.astype(f32))
    oa = _gated_deltanet(qa, ka, va, beta, g)
    oa = (oa * lax.rsqrt(jnp.mean(oa * oa, axis=-1, keepdims=True) + EPS) * gdn_norm_w.astype(f32)
          * jax.nn.silu(gate_a.astype(f32).reshape(b, s, GDN_HEADS, GDN_DV)))
    oa = oa.reshape(b, s, GDN_HEADS * GDN_DV)
    qb = _rmsnorm(qb.astype(f32).reshape(b, s, SWA_HEADS, SWA_DH), q_norm_w) * (SWA_DH ** -0.5)
    kb = _rmsnorm(kb.astype(f32).reshape(b, s, SWA_HEADS, SWA_DH), k_norm_w)
    vb = vb.astype(f32).reshape(b, s, SWA_HEADS, SWA_DH)
    ob = _dilated_attention(qb, kb, vb, rel_bias).reshape(b, s, SWA_HEADS * SWA_DH)
    h = x + jnp.concatenate([oa, ob], axis=-1).astype(x.dtype) @ w_out
    return h + _moe(_rmsnorm(h, norm2_w), w_router, b_router, w_gate_up, b_gate_up, w_down, b_down)


def setup_inputs(seed: int = 0) -> dict:
    key = jax.random.key(seed)
    ks = jax.random.split(key, 18)
    f32 = jnp.float32
    L = DEPTH
    nrm = jax.random.normal
    dt = jnp.exp(jax.random.uniform(ks[5], (L, GDN_HEADS), f32, math.log(1e-3), math.log(1e-1)))
    return {
        'x': nrm(ks[0], (BATCH, SEQ, D_MODEL), f32),
        'norm1_w': 1.0 + 0.05 * nrm(ks[1], (L, D_MODEL), f32),
        'w_in': nrm(ks[2], (L, D_MODEL, IN_DIM), f32) * D_MODEL ** -0.5,
        'conv_w': nrm(ks[3], (L, CONV_K, CONV_DIM), f32) * CONV_K ** -0.5,
        'a_log': jnp.log(jax.random.uniform(ks[4], (L, GDN_HEADS), f32, 1.0, 16.0)),
        'dt_bias': dt + jnp.log(-jnp.expm1(-dt)),
        'gdn_norm_w': 1.0 + 0.05 * nrm(ks[6], (L, GDN_DV), f32),
        'q_norm_w': 1.0 + 0.05 * nrm(ks[7], (L, SWA_DH), f32),
        'k_norm_w': 1.0 + 0.05 * nrm(ks[8], (L, SWA_DH), f32),
        'rel_bias': 0.5 * nrm(ks[9], (REL_BUCKETS, SWA_HEADS), f32),
        'w_out': nrm(ks[10], (L, MIX_WIDTH, D_MODEL), f32) * MIX_WIDTH ** -0.5,
        'norm2_w': 1.0 + 0.05 * nrm(ks[11], (L, D_MODEL), f32),
        'w_router': nrm(ks[12], (L, D_MODEL, N_EXPERTS), f32) * D_MODEL ** -0.5,
        'b_router': 0.01 * nrm(ks[13], (L, N_EXPERTS), f32),
        'w_gate_up': nrm(ks[14], (L, N_EXPERTS, D_MODEL, 2 * D_FF), f32) * D_MODEL ** -0.5,
        'b_gate_up': 0.01 * nrm(ks[15], (L, N_EXPERTS, 2 * D_FF), f32),
        'w_down': nrm(ks[16], (L, N_EXPERTS, D_FF, D_MODEL), f32) * D_FF ** -0.5,
        'b_down': 0.01 * nrm(ks[17], (L, N_EXPERTS, D_MODEL), f32),
    }


def reference(x, norm1_w, w_in, conv_w, a_log, dt_bias, gdn_norm_w, q_norm_w, k_norm_w, rel_bias,
              w_out, norm2_w, w_router, b_router, w_gate_up, b_gate_up, w_down, b_down):
    h = x
    for l in range(DEPTH):
        h = _layer(h, norm1_w[l], w_in[l], conv_w[l], a_log[l], dt_bias[l], gdn_norm_w[l],
                   q_norm_w[l], k_norm_w[l], rel_bias, w_out[l], norm2_w[l], w_router[l],
                   b_router[l], w_gate_up[l], b_gate_up[l], w_down[l], b_down[l])
    return h
```

```python
import functools
import math

import numpy as np
import jax
import jax.numpy as jnp
from jax import lax
from jax.experimental import pallas as pl
from jax.experimental.pallas import tpu as pltpu

F32 = jnp.float32
BF16 = jnp.bfloat16
I32 = jnp.int32

EPS = 1e-6
D_MODEL = 1024
GDN_HEADS = 4
GDN_DK = 128
GDN_DV = 128
CONV_K = 4
SWA_HEADS = 8
SWA_DH = 64
DILATED_BRANCHES = ((128, 1), (512, 4), (2048, 16))
SWA_BLOCK = 128
REL_BUCKETS = 32
REL_MAX_DIST = 2048
N_EXPERTS = 32
TOP_K = 4
D_FF = D_MODEL
SWIGLU_LIMIT = 7.0
SWIGLU_ALPHA = 1.702
MOE_BLOCK = 256

LANES = 128
GDN_W = GDN_HEADS * GDN_DK
SWA_W = SWA_HEADS * SWA_DH
CONV_DIM = 3 * GDN_W
COL_QKV = 0
COL_GATE = COL_QKV + CONV_DIM
COL_BA = COL_GATE + GDN_W
COL_QB = COL_BA + LANES
COL_KB = COL_QB + SWA_W
COL_VB = COL_KB + SWA_W
IN_PACKED = COL_VB + SWA_W

VMEM_LIMIT = 56 * 1024 * 1024

NEG_INF = float("-inf")


def _cparams(sem):
    return pltpu.CompilerParams(dimension_semantics=sem, vmem_limit_bytes=VMEM_LIMIT)


def _bdot(a, b):
    return jnp.dot(a.astype(BF16), b.astype(BF16), preferred_element_type=F32)


def _bdot_nt(a, b):
    return lax.dot_general(a.astype(BF16), b.astype(BF16), (((1,), (1,)), ((), ())),
                           preferred_element_type=F32)


def _bdot_tn(a, b):
    return lax.dot_general(a.astype(BF16), b.astype(BF16), (((0,), (0,)), ((), ())),
                           preferred_element_type=F32)


def _silu(x):
    return x * jax.nn.sigmoid(x)


def _in_proj_body(x_ref, nw_ref, w_ref, qnw_ref, knw_ref, hsum_ref,
                  qkv_ref, gate_ref, ba_ref, qb_ref, kb_ref, vb_ref):
    x = x_ref[...]
    ms = jnp.mean(x * x, axis=-1, keepdims=True)
    xn = (x * lax.rsqrt(ms + EPS) * nw_ref[...]).astype(BF16)

    def seg(lo, width):
        return jnp.dot(xn, w_ref[:, lo:lo + width], preferred_element_type=F32)

    qkv_ref[...] = seg(COL_QKV, CONV_DIM)
    gate_ref[...] = seg(COL_GATE, GDN_W)
    ba_ref[...] = seg(COL_BA, LANES)

    def head_rmsnorm(z, w_row, scale):
        sq = z * z
        hi = sq.astype(BF16)
        lo = (sq - hi.astype(F32)).astype(BF16)
        hmean = (jnp.dot(hi, hsum_ref[...], preferred_element_type=F32)
                 + jnp.dot(lo, hsum_ref[...], preferred_element_type=F32))
        return z * lax.rsqrt(hmean + EPS) * w_row * scale

    qb_ref[...] = head_rmsnorm(seg(COL_QB, SWA_W), qnw_ref[...], SWA_DH ** -0.5).astype(BF16)
    kb_ref[...] = head_rmsnorm(seg(COL_KB, SWA_W), knw_ref[...], 1.0).astype(BF16)
    vb_ref[...] = seg(COL_VB, SWA_W).astype(BF16)


def _in_proj(x2, norm1_w, w_packed, q_norm_w, k_norm_w, tm=512):
    t = x2.shape[0]
    hsum = jnp.asarray(np.kron(np.eye(SWA_HEADS), np.ones((SWA_DH, SWA_DH))) / SWA_DH, BF16)
    row = lambda i: (i, 0)
    full = lambda i: (0, 0)
    return pl.pallas_call(
        _in_proj_body,
        grid=(t // tm,),
        in_specs=[
            pl.BlockSpec((tm, D_MODEL), row),
            pl.BlockSpec((1, D_MODEL), full),
            pl.BlockSpec((D_MODEL, IN_PACKED), full),
            pl.BlockSpec((1, SWA_W), full),
            pl.BlockSpec((1, SWA_W), full),
            pl.BlockSpec((SWA_W, SWA_W), full),
        ],
        out_specs=[
            pl.BlockSpec((tm, CONV_DIM), row),
            pl.BlockSpec((tm, GDN_W), row),
            pl.BlockSpec((tm, LANES), row),
            pl.BlockSpec((tm, SWA_W), row),
            pl.BlockSpec((tm, SWA_W), row),
            pl.BlockSpec((tm, SWA_W), row),
        ],
        out_shape=[
            jax.ShapeDtypeStruct((t, CONV_DIM), F32),
            jax.ShapeDtypeStruct((t, GDN_W), F32),
            jax.ShapeDtypeStruct((t, LANES), F32),
            jax.ShapeDtypeStruct((t, SWA_W), BF16),
            jax.ShapeDtypeStruct((t, SWA_W), BF16),
            jax.ShapeDtypeStruct((t, SWA_W), BF16),
        ],
        compiler_params=_cparams(("parallel",)),
        name="in_proj",
    )(x2, norm1_w.reshape(1, D_MODEL), w_packed,
      jnp.tile(q_norm_w, SWA_HEADS).reshape(1, SWA_W), jnp.tile(k_norm_w, SWA_HEADS).reshape(1, SWA_W), hsum)


def _pack_w_in(w_in):
    o = np.cumsum([0, GDN_W, GDN_W, GDN_W, GDN_W, GDN_HEADS, GDN_HEADS, SWA_W, SWA_W, SWA_W])
    ba = jnp.pad(w_in[:, o[4]:o[6]], ((0, 0), (0, LANES - 2 * GDN_HEADS)))
    return jnp.concatenate([w_in[:, o[0]:o[4]], ba, w_in[:, o[6]:o[9]]], axis=1).astype(BF16)


def _unit_lower_inverse(a, c):
    ii = lax.broadcasted_iota(I32, (c, c), 0)
    jj = lax.broadcasted_iota(I32, (c, c), 1)
    eye = (ii == jj).astype(F32)
    blk = lambda m, sh: jnp.right_shift(m, sh)
    t = eye - jnp.where(blk(ii, 1) == blk(jj, 1), a, 0.0)
    sh = 1
    while (1 << sh) < c:
        e = jnp.where((blk(ii, sh + 1) == blk(jj, sh + 1)) & (blk(ii, sh) != blk(jj, sh)), a, 0.0)
        t = t - _bdot(_bdot(t, e), t)
        sh += 1
    return t


def _gdn_body(chunk, n_chunks, qkv_ref, gate_ref, ba_ref, convw_ref, alane_ref, dtlane_ref, normw_ref,
              o_ref, tail_ref, state_ref):
    c = chunk
    rows = chunk * n_chunks

    @pl.when(pl.program_id(1) == 0)
    def _():
        tail_ref[...] = jnp.zeros_like(tail_ref)
        state_ref[...] = jnp.zeros_like(state_ref)

    x = qkv_ref[0]
    xs = jnp.concatenate([tail_ref[...], x], axis=0)
    w = convw_ref[...]
    conv = (xs[5:5 + rows] * w[0:1] + xs[6:6 + rows] * w[1:2] + xs[7:7 + rows] * w[2:3] + x * w[3:4])
    tail_ref[...] = x[rows - 8:rows]
    y = _silu(conv)

    ba = ba_ref[0]
    beta_all = jax.nn.sigmoid(ba)
    z = ba + dtlane_ref[...]
    softplus = jnp.maximum(z, 0.0) + jnp.log(1.0 + jnp.exp(-jnp.abs(z)))
    g_all = -alane_ref[...] * softplus
    ri = lax.broadcasted_iota(I32, (rows, rows), 0)
    ci = lax.broadcasted_iota(I32, (rows, rows), 1)
    csh = c.bit_length() - 1
    tri = ((ri >= ci) & (jnp.right_shift(ri, csh) == jnp.right_shift(ci, csh))).astype(F32)
    gc_all = jnp.dot(tri, g_all, preferred_element_type=F32, precision=lax.Precision.HIGHEST)

    ii = lax.broadcasted_iota(I32, (c, c), 0)
    jj = lax.broadcasted_iota(I32, (c, c), 1)
    gate = gate_ref[0]
    normw = normw_ref[...]
    states = [state_ref[h] for h in range(GDN_HEADS)]
    for n in range(n_chunks):
        r0 = n * c
        gc_t = jnp.transpose(gc_all[r0:r0 + c])
        for h in range(GDN_HEADS):
            lo = h * GDN_DK
            q = y[r0:r0 + c, lo:lo + GDN_DK]
            k = y[r0:r0 + c, GDN_W + lo:GDN_W + lo + GDN_DK]
            v = y[r0:r0 + c, 2 * GDN_W + lo:2 * GDN_W + lo + GDN_DV]
            q = q * lax.rsqrt(jnp.sum(q * q, axis=-1, keepdims=True) + EPS) * (GDN_DK ** -0.5)
            k = k * lax.rsqrt(jnp.sum(k * k, axis=-1, keepdims=True) + EPS)
            beta = beta_all[r0:r0 + c, h:h + 1]
            gcol = gc_all[r0:r0 + c, GDN_HEADS + h:GDN_HEADS + h + 1]
            grow = gc_t[GDN_HEADS + h:GDN_HEADS + h + 1, :]
            glast = gc_all[r0 + c - 1:r0 + c, GDN_HEADS + h:GDN_HEADS + h + 1]
            decay = jnp.exp(jnp.where(ii >= jj, gcol - grow, NEG_INF))
            kb = k * beta
            a_kk = jnp.where(ii > jj, _bdot_nt(kb, k) * decay, 0.0)
            t_inv = _unit_lower_inverse(a_kk, c)
            egc = jnp.exp(gcol)
            sol = _bdot(t_inv, jnp.concatenate([v * beta, kb * egc], axis=-1))
            u, wk = sol[:, :GDN_DV], sol[:, GDN_DV:]
            a_qk = _bdot_nt(q, k) * decay
            q_dec = q * egc
            k_dec = k * jnp.exp(glast - gcol)
            s = states[h]
            v_new = u - _bdot(wk, s)
            o = _bdot(q_dec, s) + _bdot(a_qk, v_new)
            states[h] = s * jnp.exp(glast) + _bdot_tn(k_dec, v_new)
            o = (o * lax.rsqrt(jnp.mean(o * o, axis=-1, keepdims=True) + EPS) * normw
                 * _silu(gate[r0:r0 + c, lo:lo + GDN_DV]))
            o_ref[0, r0:r0 + c, lo:lo + GDN_DV] = o
    for h in range(GDN_HEADS):
        state_ref[h] = states[h]


def _gdn(qkv, gate, ba, conv_w, a_log, dt_bias, gdn_norm_w, chunk=64, n_chunks=4):
    b, s, _ = qkv.shape
    rows = chunk * n_chunks
    pad = jnp.zeros((LANES - 2 * GDN_HEADS,), F32)
    alane = jnp.concatenate([jnp.zeros((GDN_HEADS,), F32), jnp.exp(a_log.astype(F32)), pad]).reshape(1, LANES)
    dtlane = jnp.concatenate([jnp.zeros((GDN_HEADS,), F32), dt_bias.astype(F32), pad]).reshape(1, LANES)
    blk = lambda i, j: (i, j, 0)
    full = lambda i, j: (0, 0)
    return pl.pallas_call(
        functools.partial(_gdn_body, chunk, n_chunks),
        grid=(b, s // rows),
        in_specs=[
            pl.BlockSpec((1, rows, CONV_DIM), blk),
            pl.BlockSpec((1, rows, GDN_W), blk),
            pl.BlockSpec((1, rows, LANES), blk),
            pl.BlockSpec((CONV_K, CONV_DIM), full),
            pl.BlockSpec((1, LANES), full),
            pl.BlockSpec((1, LANES), full),
            pl.BlockSpec((1, GDN_DV), full),
        ],
        out_specs=pl.BlockSpec((1, rows, GDN_W), blk),
        out_shape=jax.ShapeDtypeStruct((b, s, GDN_W), F32),
        scratch_shapes=[pltpu.VMEM((8, CONV_DIM), F32), pltpu.VMEM((GDN_HEADS, GDN_DK, GDN_DV), F32)],
        compiler_params=_cparams(("parallel", "arbitrary")),
        name="gdn",
    )(qkv, gate, ba, conv_w.astype(F32), alane, dtlane, gdn_norm_w.astype(F32).reshape(1, GDN_DV))


def _t5_bucket(dist):
    max_exact = REL_BUCKETS // 2
    n = np.maximum(dist, 0)
    large = max_exact + (np.log(np.maximum(n, 1) / max_exact) / math.log(REL_MAX_DIST / max_exact)
                         * (REL_BUCKETS - max_exact)).astype(np.int32)
    large = np.minimum(large, REL_BUCKETS - 1)
    return np.where(n < max_exact, n, large).astype(np.int32)


def _branch_bias(rel_bias, window, dilation):
    qi = np.arange(SWA_BLOCK)[:, None]
    kj = np.arange(2 * SWA_BLOCK)[None, :]
    steps = qi + SWA_BLOCK - kj
    valid = (steps >= 0) & (steps <= window // dilation)
    bias = rel_bias.astype(F32)[jnp.asarray(_t5_bucket(steps * dilation))].transpose(2, 0, 1)
    inner = jnp.where(jnp.asarray(valid)[None], bias, NEG_INF)
    first = jnp.where(jnp.asarray(valid & (kj >= SWA_BLOCK))[None], bias, NEG_INF)
    return jnp.stack([inner, first])


def _swa_body(group, has_acc, *refs):
    if has_acc:
        q_ref, kp_ref, kc_ref, vp_ref, vc_ref, bias_ref, acco_ref, accl_ref, o_ref, l_ref = refs
    else:
        q_ref, kp_ref, kc_ref, vp_ref, vc_ref, bias_ref, o_ref, l_ref = refs
    blk = SWA_BLOCK
    first_step = pl.program_id(2) == 0
    k_all = jnp.concatenate([kp_ref[0], kc_ref[0]], axis=0)
    v_all = jnp.concatenate([vp_ref[0], vc_ref[0]], axis=0)
    lane = lax.broadcasted_iota(I32, (blk, LANES), 1)
    for g in range(group):
        r0 = g * blk
        qg = q_ref[0, r0:r0 + blk, :]
        kg = k_all[r0:r0 + 2 * blk]
        vg = v_all[r0:r0 + 2 * blk]
        bsel = jnp.where(first_step, 1, 0) if g == 0 else 0
        outs = []
        lse_tile = jnp.zeros((blk, LANES), F32)
        for h in range(SWA_HEADS):
            lo = h * SWA_DH
            sc = _bdot_nt(qg[:, lo:lo + SWA_DH], kg[:, lo:lo + SWA_DH]) + bias_ref[bsel, h]
            mx = jnp.max(sc, axis=-1, keepdims=True)
            p = jnp.exp(sc - mx)
            den = jnp.sum(p, axis=-1, keepdims=True)
            o = _bdot(p, vg[:, lo:lo + SWA_DH]) / den
            lse = mx + jnp.log(den)
            if has_acc:
                lse_a = accl_ref[0, r0:r0 + blk, h:h + 1]
                o_a = acco_ref[0, r0:r0 + blk, lo:lo + SWA_DH]
                m2 = jnp.maximum(lse_a, lse)
                ea = jnp.exp(lse_a - m2)
                eb = jnp.exp(lse - m2)
                o = (o_a * ea + o * eb) / (ea + eb)
                lse = m2 + jnp.log(ea + eb)
            outs.append(o)
            lse_tile = jnp.where(lane == h, lse, lse_tile)
        o_ref[0, r0:r0 + blk, :] = jnp.concatenate(outs, axis=-1)
        l_ref[0, r0:r0 + blk, :] = lse_tile


def _swa_branch(qb, kb, vb, bias2, dilation, acc):
    b, s, _ = qb.shape
    d = dilation
    length = s // d
    nb = length // SWA_BLOCK
    group = min(4, nb)
    rows = group * SWA_BLOCK
    view = lambda a, wdt: a.reshape(b, length, d * wdt)
    cur = lambda i, r, j: (i, j, r)
    prev = lambda i, r, j: (i, jnp.maximum(j * group - 1, 0), r)
    in_specs = [
        pl.BlockSpec((1, rows, SWA_W), cur),
        pl.BlockSpec((1, SWA_BLOCK, SWA_W), prev),
        pl.BlockSpec((1, rows, SWA_W), cur),
        pl.BlockSpec((1, SWA_BLOCK, SWA_W), prev),
        pl.BlockSpec((1, rows, SWA_W), cur),
        pl.BlockSpec((2, SWA_HEADS, SWA_BLOCK, 2 * SWA_BLOCK), lambda i, r, j: (0, 0, 0, 0)),
    ]
    args = [view(qb, SWA_W), view(kb, SWA_W), view(kb, SWA_W), view(vb, SWA_W), view(vb, SWA_W), bias2]
    if acc is not None:
        in_specs += [pl.BlockSpec((1, rows, SWA_W), cur), pl.BlockSpec((1, rows, LANES), cur)]
        args += [view(acc[0], SWA_W), view(acc[1], LANES)]
    o, lse = pl.pallas_call(
        functools.partial(_swa_body, group, acc is not None),
        grid=(b, d, nb // group),
        in_specs=in_specs,
        out_specs=[pl.BlockSpec((1, rows, SWA_W), cur), pl.BlockSpec((1, rows, LANES), cur)],
        out_shape=[jax.ShapeDtypeStruct((b, length, d * SWA_W), F32),
                   jax.ShapeDtypeStruct((b, length, d * LANES), F32)],
        compiler_params=_cparams(("parallel", "parallel", "arbitrary")),
        name=f"swa_d{d}",
    )(*args)
    return o.reshape(b, s, SWA_W), lse.reshape(b, s, LANES)


def _dilated_attention(qb, kb, vb, rel_bias):
    acc = None
    for window, dilation in DILATED_BRANCHES:
        acc = _swa_branch(qb, kb, vb, _branch_bias(rel_bias, window, dilation), dilation, acc)
    return acc[0]


def _out_proj_body(x_ref, oa_ref, ob_ref, w_ref, nw_ref, wr_ref, br_ref, h_ref, hn_ref, e_ref, g_ref):
    h = (x_ref[...] + _bdot(oa_ref[...], w_ref[0:GDN_W, :]) + _bdot(ob_ref[...], w_ref[GDN_W:, :]))
    h_ref[...] = h
    hn = h * lax.rsqrt(jnp.mean(h * h, axis=-1, keepdims=True) + EPS) * nw_ref[...]
    hn_ref[...] = hn
    logits = jnp.dot(hn, wr_ref[...], preferred_element_type=F32, precision=lax.Precision.HIGHEST) + br_ref[...]
    lane = lax.broadcasted_iota(I32, logits.shape, 1)
    e_tile = jnp.zeros(logits.shape, I32)
    g_tile = jnp.zeros(logits.shape, F32)
    top0 = None
    den = None
    for k in range(TOP_K):
        mx = jnp.max(logits, axis=-1, keepdims=True)
        idx = jnp.min(jnp.where(logits == mx, lane, LANES), axis=-1, keepdims=True)
        logits = jnp.where(lane == idx, NEG_INF, logits)
        if k == 0:
            top0 = mx
        ex = jnp.exp(mx - top0)
        den = ex if k == 0 else den + ex
        e_tile = jnp.where(lane == k, idx, e_tile)
        g_tile = jnp.where(lane == k, ex, g_tile)
    e_ref[...] = e_tile
    g_ref[...] = g_tile / den


def _out_proj(x2, oa, ob, w_out, norm2_w, w_router, b_router, tm=512):
    t = x2.shape[0]
    wr = jnp.pad(w_router.astype(F32), ((0, 0), (0, LANES - N_EXPERTS)))
    br = jnp.concatenate([b_router.astype(F32), jnp.full((LANES - N_EXPERTS,), NEG_INF, F32)]).reshape(1, LANES)
    row = lambda i: (i, 0)
    full = lambda i: (0, 0)
    return pl.pallas_call(
        _out_proj_body,
        grid=(t // tm,),
        in_specs=[
            pl.BlockSpec((tm, D_MODEL), row),
            pl.BlockSpec((tm, GDN_W), row),
            pl.BlockSpec((tm, SWA_W), row),
            pl.BlockSpec((GDN_W + SWA_W, D_MODEL), full),
            pl.BlockSpec((1, D_MODEL), full),
            pl.BlockSpec((D_MODEL, LANES), full),
            pl.BlockSpec((1, LANES), full),
        ],
        out_specs=[
            pl.BlockSpec((tm, D_MODEL), row),
            pl.BlockSpec((tm, D_MODEL), row),
            pl.BlockSpec((tm, LANES), row),
            pl.BlockSpec((tm, LANES), row),
        ],
        out_shape=[
            jax.ShapeDtypeStruct((t, D_MODEL), F32),
            jax.ShapeDtypeStruct((t, D_MODEL), F32),
            jax.ShapeDtypeStruct((t, LANES), I32),
            jax.ShapeDtypeStruct((t, LANES), F32),
        ],
        compiler_params=_cparams(("parallel",)),
        name="out_proj",
    )(x2, oa, ob, w_out.astype(BF16), norm2_w.astype(F32).reshape(1, D_MODEL), wr, br)


def _moe_plan(top_e):
    t = top_e.shape[0]
    n = t * TOP_K
    p_rows = n + N_EXPERTS * MOE_BLOCK
    nblk = p_rows // MOE_BLOCK
    e_flat = top_e.reshape(-1)
    onehot = (e_flat[:, None] == jnp.arange(N_EXPERTS, dtype=I32)[None, :]).astype(I32)
    csum = jnp.cumsum(onehot, axis=0)
    rank = jnp.sum(csum * onehot, axis=1) - 1
    counts = csum[-1]
    padded = (counts + MOE_BLOCK - 1) // MOE_BLOCK * MOE_BLOCK
    pad_end = jnp.cumsum(padded)
    pad_start = pad_end - padded
    dest = pad_start[e_flat] + rank
    src = jnp.full((p_rows,), -1, I32).at[dest].set(jnp.arange(n, dtype=I32))
    is_pad = src < 0
    pad_rank = jnp.cumsum(is_pad.astype(I32)) - 1
    gather_tok = jnp.where(is_pad, 0, src // TOP_K)
    scatter_row = jnp.where(is_pad, n + pad_rank, src)
    blk_start = jnp.arange(nblk, dtype=I32) * MOE_BLOCK
    blk_expert = jnp.minimum(jnp.sum((blk_start[:, None] >= pad_end[None, :]).astype(I32), axis=1), N_EXPERTS - 1)
    n_used = pad_end[-1] // MOE_BLOCK
    return gather_tok.reshape(nblk, 1, MOE_BLOCK), scatter_row.reshape(nblk, 1, MOE_BLOCK), blk_expert, n_used


def _moe_body(bexp_ref, nused_ref, gidx_ref, sidx_ref, hn_ref, wgu_ref, bgu_ref, wdn_ref, bdn_ref, y_ref,
              xbuf, ybuf, gsem, ssem):
    i = pl.program_id(0)
    n_used = nused_ref[0]
    slot = i % 2

    def gather_copy(tok, s, r):
        return pltpu.make_async_copy(hn_ref.at[pl.ds(tok, 1)], xbuf.at[s, pl.ds(r, 1)], gsem.at[s])

    def scatter_copy(row, s, r):
        return pltpu.make_async_copy(ybuf.at[s, pl.ds(r, 1)], y_ref.at[pl.ds(row, 1)], ssem.at[s])

    @pl.when(i == 0)
    def _():
        for r in range(MOE_BLOCK):
            gather_copy(gidx_ref[0, 0, r], 0, r).start()

    @pl.when(i >= 2)
    def _():
        for r in range(MOE_BLOCK):
            scatter_copy(0, slot, r).wait()

    @pl.when(i < n_used)
    def _():
        @pl.when(i + 1 < n_used)
        def _():
            for r in range(MOE_BLOCK):
                gather_copy(gidx_ref[1, 0, r], 1 - slot, r).start()

        for r in range(MOE_BLOCK):
            gather_copy(0, slot, r).wait()
        x = xbuf[slot]
        gu = _bdot(x, wgu_ref[0]) + bgu_ref[0]
        gate = jnp.minimum(gu[:, :D_FF], SWIGLU_LIMIT)
        up = jnp.clip(gu[:, D_FF:], -SWIGLU_LIMIT, SWIGLU_LIMIT)
        act = (up + 1.0) * gate * jax.nn.sigmoid(SWIGLU_ALPHA * gate)
        ybuf[slot] = _bdot(act, wdn_ref[0]) + bdn_ref[0]

    @pl.when(i >= n_used)
    def _():
        ybuf[slot] = jnp.zeros((MOE_BLOCK, D_MODEL), F32)

    for r in range(MOE_BLOCK):
        scatter_copy(sidx_ref[0, 0, r], slot, r).start()

    @pl.when(i == pl.num_programs(0) - 1)
    def _():
        for r in range(MOE_BLOCK):
            scatter_copy(0, slot, r).wait()
        for r in range(MOE_BLOCK):
            scatter_copy(0, 1 - slot, r).wait()


def _moe_ffn(hn, plan, w_gu, b_gu, w_dn, b_dn):
    t = hn.shape[0]
    gather_tok, scatter_row, blk_expert, n_used = plan
    nblk = gather_tok.shape[0]
    n_rows = t * TOP_K + N_EXPERTS * MOE_BLOCK
    gnext = jnp.concatenate([gather_tok[1:], gather_tok[-1:]], axis=0)
    gpair = jnp.concatenate([gather_tok, gnext], axis=1).reshape(nblk * 2, 1, MOE_BLOCK)
    wsel = lambda i, be, nu: (be[i], 0, 0)
    grid_spec = pltpu.PrefetchScalarGridSpec(
        num_scalar_prefetch=2,
        grid=(nblk,),
        in_specs=[
            pl.BlockSpec((2, 1, MOE_BLOCK), lambda i, be, nu: (i, 0, 0), memory_space=pltpu.SMEM),
            pl.BlockSpec((1, 1, MOE_BLOCK), lambda i, be, nu: (i, 0, 0), memory_space=pltpu.SMEM),
            pl.BlockSpec(memory_space=pl.ANY),
            pl.BlockSpec((1, D_MODEL, 2 * D_FF), wsel),
            pl.BlockSpec((1, 1, 2 * D_FF), wsel),
            pl.BlockSpec((1, D_FF, D_MODEL), wsel),
            pl.BlockSpec((1, 1, D_MODEL), wsel),
        ],
        out_specs=pl.BlockSpec(memory_space=pl.ANY),
        scratch_shapes=[
            pltpu.VMEM((2, MOE_BLOCK, D_MODEL), F32),
            pltpu.VMEM((2, MOE_BLOCK, D_MODEL), F32),
            pltpu.SemaphoreType.DMA((2,)),
            pltpu.SemaphoreType.DMA((2,)),
        ],
    )
    return pl.pallas_call(
        _moe_body,
        grid_spec=grid_spec,
        out_shape=jax.ShapeDtypeStruct((n_rows, D_MODEL), F32),
        compiler_params=_cparams(("arbitrary",)),
        name="moe_ffn",
    )(blk_expert, n_used.reshape(1), gpair, scatter_row, hn,
      w_gu.astype(BF16), b_gu.astype(F32).reshape(N_EXPERTS, 1, 2 * D_FF),
      w_dn.astype(BF16), b_dn.astype(F32).reshape(N_EXPERTS, 1, D_MODEL))


def _combine_body(h_ref, y_ref, g_ref, o_ref):
    acc = h_ref[...]
    g = g_ref[...]
    moe = g[:, 0:1] * y_ref[:, 0:D_MODEL]
    for k in range(1, TOP_K):
        moe = moe + g[:, k:k + 1] * y_ref[:, k * D_MODEL:(k + 1) * D_MODEL]
    o_ref[...] = acc + moe


def _combine(h, y_slots, gates, tm=512):
    t = h.shape[0]
    n_rows = y_slots.shape[0]
    yv = y_slots.reshape(n_rows // TOP_K, TOP_K * D_MODEL)
    row = lambda i: (i, 0)
    return pl.pallas_call(
        _combine_body,
        grid=(t // tm,),
        in_specs=[pl.BlockSpec((tm, D_MODEL), row), pl.BlockSpec((tm, TOP_K * D_MODEL), row),
                  pl.BlockSpec((tm, LANES), row)],
        out_specs=pl.BlockSpec((tm, D_MODEL), row),
        out_shape=jax.ShapeDtypeStruct((t, D_MODEL), F32),
        compiler_params=_cparams(("parallel",)),
        name="combine",
    )(h, yv, gates)


def _layer(x, norm1_w, w_in, conv_w, a_log, dt_bias, gdn_norm_w, q_norm_w, k_norm_w, rel_bias,
           w_out, norm2_w, w_router, b_router, w_gate_up, b_gate_up, w_down, b_down):
    b, s, d = x.shape
    t = b * s
    x2 = x.reshape(t, d)
    qkv, gate, ba, qb, kb, vb = _in_proj(x2, norm1_w.astype(F32), _pack_w_in(w_in),
                                         q_norm_w.astype(F32), k_norm_w.astype(F32))
    oa = _gdn(qkv.reshape(b, s, CONV_DIM), gate.reshape(b, s, GDN_W), ba.reshape(b, s, LANES),
              conv_w, a_log, dt_bias, gdn_norm_w)
    ob = _dilated_attention(qb.reshape(b, s, SWA_W), kb.reshape(b, s, SWA_W), vb.reshape(b, s, SWA_W), rel_bias)
    h, hn, top_e, gates = _out_proj(x2, oa.reshape(t, GDN_W), ob.reshape(t, SWA_W), w_out, norm2_w,
                                    w_router, b_router)
    plan = _moe_plan(top_e[:, :TOP_K])
    y_slots = _moe_ffn(hn, plan, w_gate_up, b_gate_up, w_down, b_down)
    return _combine(h, y_slots, gates).reshape(b, s, d)


def kernel(x, norm1_w, w_in, conv_w, a_log, dt_bias, gdn_norm_w, q_norm_w, k_norm_w, rel_bias, w_out, norm2_w,
           w_router, b_router, w_gate_up, b_gate_up, w_down, b_down):
    h = x
    for l in range(norm1_w.shape[0]):
        h = _layer(h, norm1_w[l], w_in[l], conv_w[l], a_log[l], dt_bias[l], gdn_norm_w[l], q_norm_w[l],
                   k_norm_w[l], rel_bias, w_out[l], norm2_w[l], w_router[l], b_router[l], w_gate_up[l],
                   b_gate_up[l], w_down[l], b_down[l])
    return h
```

```python
import functools
import math

import numpy as np
import jax
import jax.numpy as jnp
from jax import lax
from jax.experimental import pallas as pl
from jax.experimental.pallas import tpu as pltpu

F32 = jnp.float32
BF16 = jnp.bfloat16
I32 = jnp.int32

EPS = 1e-6
D_MODEL = 1024
GDN_HEADS = 4
GDN_DK = 128
GDN_DV = 128
CONV_K = 4
SWA_HEADS = 8
SWA_DH = 64
DILATED_BRANCHES = ((128, 1), (512, 4), (2048, 16))
SWA_BLOCK = 128
REL_BUCKETS = 32
REL_MAX_DIST = 2048
N_EXPERTS = 32
TOP_K = 4
D_FF = D_MODEL
SWIGLU_LIMIT = 7.0
SWIGLU_ALPHA = 1.702
MOE_BLOCK = 256

LANES = 128
GDN_W = GDN_HEADS * GDN_DK
SWA_W = SWA_HEADS * SWA_DH
SWA_SLABS = SWA_W // LANES
CONV_DIM = 3 * GDN_W
COL_QKV = 0
COL_GATE = COL_QKV + CONV_DIM
COL_BA = COL_GATE + GDN_W
COL_QB = COL_BA + LANES
COL_KB = COL_QB + SWA_W
COL_VB = COL_KB + SWA_W
IN_PACKED = COL_VB + SWA_W

VMEM_LIMIT = 56 * 1024 * 1024

NEG_INF = float("-inf")


def _cparams(sem):
    return pltpu.CompilerParams(dimension_semantics=sem, vmem_limit_bytes=VMEM_LIMIT)


def _bdot(a, b):
    return jnp.dot(a.astype(BF16), b.astype(BF16), preferred_element_type=F32)


def _bdot_nt(a, b):
    return lax.dot_general(a.astype(BF16), b.astype(BF16), (((1,), (1,)), ((), ())),
                           preferred_element_type=F32)


def _bdot_tn(a, b):
    return lax.dot_general(a.astype(BF16), b.astype(BF16), (((0,), (0,)), ((), ())),
                           preferred_element_type=F32)


def _silu(x):
    return x * jax.nn.sigmoid(x)


def _in_proj_body(x_ref, nw_ref, w_ref, qnw_ref, knw_ref, hsum_ref,
                  qkv_ref, gate_ref, ba_ref, qb_ref, kb_ref, vb_ref):
    x = x_ref[...]
    ms = jnp.mean(x * x, axis=-1, keepdims=True)
    xn = (x * lax.rsqrt(ms + EPS) * nw_ref[...]).astype(BF16)

    def seg(lo, width):
        return jnp.dot(xn, w_ref[:, lo:lo + width], preferred_element_type=F32)

    qkv_ref[...] = seg(COL_QKV, CONV_DIM)
    gate_ref[...] = seg(COL_GATE, GDN_W)
    ba_ref[...] = seg(COL_BA, LANES)

    def head_rmsnorm(z, w_row, scale):
        sq = z * z
        hi = sq.astype(BF16)
        lo = (sq - hi.astype(F32)).astype(BF16)
        hmean = (jnp.dot(hi, hsum_ref[...], preferred_element_type=F32)
                 + jnp.dot(lo, hsum_ref[...], preferred_element_type=F32))
        return z * lax.rsqrt(hmean + EPS) * w_row * scale

    def put_slabs(ref, z):
        for j in range(SWA_SLABS):
            ref[j] = z[:, j * LANES:(j + 1) * LANES]

    put_slabs(qb_ref, head_rmsnorm(seg(COL_QB, SWA_W), qnw_ref[...], SWA_DH ** -0.5))
    put_slabs(kb_ref, head_rmsnorm(seg(COL_KB, SWA_W), knw_ref[...], 1.0))
    put_slabs(vb_ref, seg(COL_VB, SWA_W))


def _in_proj(x2, norm1_w, w_packed, q_norm_w, k_norm_w, tm=512):
    t = x2.shape[0]
    hsum = jnp.asarray(np.kron(np.eye(SWA_HEADS), np.ones((SWA_DH, SWA_DH))) / SWA_DH, BF16)
    row = lambda i: (i, 0)
    full = lambda i: (0, 0)
    slab = lambda i: (0, i, 0)
    return pl.pallas_call(
        _in_proj_body,
        grid=(t // tm,),
        in_specs=[
            pl.BlockSpec((tm, D_MODEL), row),
            pl.BlockSpec((1, D_MODEL), full),
            pl.BlockSpec((D_MODEL, IN_PACKED), full),
            pl.BlockSpec((1, SWA_W), full),
            pl.BlockSpec((1, SWA_W), full),
            pl.BlockSpec((SWA_W, SWA_W), full),
        ],
        out_specs=[
            pl.BlockSpec((tm, CONV_DIM), row),
            pl.BlockSpec((tm, GDN_W), row),
            pl.BlockSpec((tm, LANES), row),
            pl.BlockSpec((SWA_SLABS, tm, LANES), slab),
            pl.BlockSpec((SWA_SLABS, tm, LANES), slab),
            pl.BlockSpec((SWA_SLABS, tm, LANES), slab),
        ],
        out_shape=[
            jax.ShapeDtypeStruct((t, CONV_DIM), F32),
            jax.ShapeDtypeStruct((t, GDN_W), F32),
            jax.ShapeDtypeStruct((t, LANES), F32),
            jax.ShapeDtypeStruct((SWA_SLABS, t, LANES), F32),
            jax.ShapeDtypeStruct((SWA_SLABS, t, LANES), F32),
            jax.ShapeDtypeStruct((SWA_SLABS, t, LANES), F32),
        ],
        compiler_params=_cparams(("parallel",)),
        name="in_proj",
    )(x2, norm1_w.reshape(1, D_MODEL), w_packed,
      jnp.tile(q_norm_w, SWA_HEADS).reshape(1, SWA_W), jnp.tile(k_norm_w, SWA_HEADS).reshape(1, SWA_W), hsum)


def _pack_w_in(w_in):
    o = np.cumsum([0, GDN_W, GDN_W, GDN_W, GDN_W, GDN_HEADS, GDN_HEADS, SWA_W, SWA_W, SWA_W])
    ba = jnp.pad(w_in[:, o[4]:o[6]], ((0, 0), (0, LANES - 2 * GDN_HEADS)))
    return jnp.concatenate([w_in[:, o[0]:o[4]], ba, w_in[:, o[6]:o[9]]], axis=1).astype(BF16)


def _unit_lower_inverses(a_list, n, blk):
    ii = lax.broadcasted_iota(I32, (n, n), 0)
    jj = lax.broadcasted_iota(I32, (n, n), 1)
    same = lambda sh: jnp.right_shift(ii, sh) == jnp.right_shift(jj, sh)
    eye = (ii == jj).astype(F32)
    pair = same(1)
    t_list = [eye - jnp.where(pair, a, 0.0) for a in a_list]
    sh = 1
    while (1 << sh) < blk:
        join = same(sh + 1) & jnp.logical_not(same(sh))
        te = [_bdot(t, jnp.where(join, a, 0.0)) for t, a in zip(t_list, a_list)]
        tet = [_bdot(x, t) for x, t in zip(te, t_list)]
        t_list = [t - y for t, y in zip(t_list, tet)]
        sh += 1
    return t_list


def _gdn_body(chunk, n_chunks, qkv_ref, gate_ref, ba_ref, convw_ref, alane_ref, dtlane_ref, normw_ref,
              o_ref, tail_ref, state_ref):
    c = chunk
    rows = chunk * n_chunks

    @pl.when(pl.program_id(1) == 0)
    def _():
        tail_ref[...] = jnp.zeros_like(tail_ref)
        state_ref[...] = jnp.zeros_like(state_ref)

    x = qkv_ref[0]
    xs = jnp.concatenate([tail_ref[...], x], axis=0)
    w = convw_ref[...]
    conv = (xs[5:5 + rows] * w[0:1] + xs[6:6 + rows] * w[1:2] + xs[7:7 + rows] * w[2:3] + x * w[3:4])
    tail_ref[...] = x[rows - 8:rows]
    y = _silu(conv)

    ba = ba_ref[0]
    beta_all = jax.nn.sigmoid(ba)
    z = ba + dtlane_ref[...]
    softplus = jnp.maximum(z, 0.0) + jnp.log(1.0 + jnp.exp(-jnp.abs(z)))
    g_all = -alane_ref[...] * softplus
    ri = lax.broadcasted_iota(I32, (rows, rows), 0)
    ci = lax.broadcasted_iota(I32, (rows, rows), 1)
    csh = c.bit_length() - 1
    tri = ((ri >= ci) & (jnp.right_shift(ri, csh) == jnp.right_shift(ci, csh))).astype(F32)
    gc_all = jnp.dot(tri, g_all, preferred_element_type=F32, precision=lax.Precision.HIGHEST)

    heads = range(GDN_HEADS)
    qn, kn, vv = [], [], []
    for h in heads:
        lo = h * GDN_DK
        q = y[:, lo:lo + GDN_DK]
        k = y[:, GDN_W + lo:GDN_W + lo + GDN_DK]
        qn.append(q * lax.rsqrt(jnp.sum(q * q, axis=-1, keepdims=True) + EPS) * (GDN_DK ** -0.5))
        kn.append(k * lax.rsqrt(jnp.sum(k * k, axis=-1, keepdims=True) + EPS))
        vv.append(y[:, 2 * GDN_W + lo:2 * GDN_W + lo + GDN_DV])

    hs = GDN_HEADS * c
    hsh = csh
    ii = lax.broadcasted_iota(I32, (hs, hs), 0)
    jj = lax.broadcasted_iota(I32, (hs, hs), 1)
    same_head = jnp.right_shift(ii, hsh) == jnp.right_shift(jj, hsh)
    row_head = jnp.right_shift(lax.broadcasted_iota(I32, (hs, GDN_DV), 0), hsh)

    def stack(parts, r0):
        return jnp.concatenate([p[r0:r0 + c] for p in parts], axis=0)

    def col(src, lane0, r0):
        return jnp.concatenate([src[r0:r0 + c, lane0 + h:lane0 + h + 1] for h in heads], axis=0)

    def diag_blocks(p):
        out = jnp.where(row_head == 0, p[:, 0:GDN_DV], 0.0)
        for h in range(1, GDN_HEADS):
            out = out + jnp.where(row_head == h, p[:, h * GDN_DV:(h + 1) * GDN_DV], 0.0)
        return out

    q_st, u_rhs, kdec, cdecay, a_kk, a_qk = ([] for _ in range(6))
    for n in range(n_chunks):
        r0 = n * c
        k = stack(kn, r0)
        q = stack(qn, r0)
        v = stack(vv, r0)
        beta = col(beta_all, 0, r0)
        gcol = col(gc_all, GDN_HEADS, r0)
        grow = jnp.transpose(jnp.broadcast_to(gcol, (hs, LANES)))[0:1, :]
        glast = jnp.concatenate(
            [jnp.broadcast_to(gc_all[r0 + c - 1:r0 + c, GDN_HEADS + h:GDN_HEADS + h + 1], (c, 1)) for h in heads],
            axis=0)
        dec = jnp.exp(jnp.where(same_head & (ii >= jj), gcol - grow, NEG_INF))
        eg = jnp.exp(gcol)
        kb = k * beta
        prod = _bdot_nt(jnp.concatenate([kb, q], axis=0), k)
        a_kk.append(jnp.where(ii > jj, prod[:hs] * dec, 0.0))
        a_qk.append(prod[hs:] * dec)
        q_st.append(q * eg)
        u_rhs.append(jnp.concatenate([v * beta, kb * eg], axis=-1))
        kdec.append(k * jnp.exp(glast - gcol))
        cdecay.append(jnp.concatenate(
            [jnp.broadcast_to(jnp.exp(gc_all[r0 + c - 1:r0 + c, GDN_HEADS + h:GDN_HEADS + h + 1]), (1, GDN_DV))
             for h in heads], axis=-1))
    t_inv = _unit_lower_inverses(a_kk, hs, c)
    sols = [_bdot(t, r) for t, r in zip(t_inv, u_rhs)]

    gate = gate_ref[0]
    normw = normw_ref[...]
    s_cat = state_ref[...]
    for n in range(n_chunks):
        r0 = n * c
        u, wk = sols[n][:, :GDN_DV], sols[n][:, GDN_DV:]
        p = _bdot(jnp.concatenate([wk, q_st[n]], axis=0), s_cat)
        v_new = u - diag_blocks(p[:hs])
        o = diag_blocks(p[hs:]) + _bdot(a_qk[n], v_new)
        v_bd = jnp.concatenate([jnp.where(row_head == h, v_new, 0.0) for h in heads], axis=-1)
        s_cat = s_cat * cdecay[n] + _bdot_tn(kdec[n], v_bd)
        o = o * lax.rsqrt(jnp.mean(o * o, axis=-1, keepdims=True) + EPS) * normw
        for h in heads:
            lo = h * GDN_DV
            o_ref[0, r0:r0 + c, lo:lo + GDN_DV] = o[h * c:(h + 1) * c] * _silu(gate[r0:r0 + c, lo:lo + GDN_DV])
    state_ref[...] = s_cat


def _gdn(qkv, gate, ba, conv_w, a_log, dt_bias, gdn_norm_w, chunk=64, n_chunks=4):
    b, s, _ = qkv.shape
    rows = chunk * n_chunks
    pad = jnp.zeros((LANES - 2 * GDN_HEADS,), F32)
    alane = jnp.concatenate([jnp.zeros((GDN_HEADS,), F32), jnp.exp(a_log.astype(F32)), pad]).reshape(1, LANES)
    dtlane = jnp.concatenate([jnp.zeros((GDN_HEADS,), F32), dt_bias.astype(F32), pad]).reshape(1, LANES)
    blk = lambda i, j: (i, j, 0)
    full = lambda i, j: (0, 0)
    return pl.pallas_call(
        functools.partial(_gdn_body, chunk, n_chunks),
        grid=(b, s // rows),
        in_specs=[
            pl.BlockSpec((1, rows, CONV_DIM), blk),
            pl.BlockSpec((1, rows, GDN_W), blk),
            pl.BlockSpec((1, rows, LANES), blk),
            pl.BlockSpec((CONV_K, CONV_DIM), full),
            pl.BlockSpec((1, LANES), full),
            pl.BlockSpec((1, LANES), full),
            pl.BlockSpec((1, GDN_DV), full),
        ],
        out_specs=pl.BlockSpec((1, rows, GDN_W), blk),
        out_shape=jax.ShapeDtypeStruct((b, s, GDN_W), F32),
        scratch_shapes=[pltpu.VMEM((8, CONV_DIM), F32), pltpu.VMEM((GDN_DK, GDN_HEADS * GDN_DV), F32)],
        compiler_params=_cparams(("parallel", "arbitrary")),
        name="gdn",
    )(qkv, gate, ba, conv_w.astype(F32), alane, dtlane, gdn_norm_w.astype(F32).reshape(1, GDN_DV))


def _t5_bucket(dist):
    max_exact = REL_BUCKETS // 2
    n = np.maximum(dist, 0)
    large = max_exact + (np.log(np.maximum(n, 1) / max_exact) / math.log(REL_MAX_DIST / max_exact)
                         * (REL_BUCKETS - max_exact)).astype(np.int32)
    large = np.minimum(large, REL_BUCKETS - 1)
    return np.where(n < max_exact, n, large).astype(np.int32)


def _branch_bias(rel_bias, window, dilation):
    kj = np.arange(2 * SWA_BLOCK)[:, None]
    qi = np.arange(SWA_BLOCK)[None, :]
    steps = qi + SWA_BLOCK - kj
    valid = (steps >= 0) & (steps <= window // dilation)
    onehot = np.eye(REL_BUCKETS, dtype=np.float32)[_t5_bucket(steps * dilation).reshape(-1)]
    bias = jnp.dot(jnp.asarray(onehot), rel_bias.astype(F32), precision=lax.Precision.HIGHEST)
    bias = bias.reshape(2 * SWA_BLOCK, SWA_BLOCK, SWA_HEADS).transpose(2, 0, 1)
    bias = jnp.where(jnp.asarray(valid)[None], bias, NEG_INF)
    return jnp.concatenate([bias[:, :SWA_BLOCK], jnp.full((SWA_HEADS, SWA_BLOCK, SWA_BLOCK), NEG_INF, F32),
                            bias[:, SWA_BLOCK:]], axis=1)


SWA_SUPER = 2048
STAT_ROWS = 16


def _swa_body(q_ref, k_ref, v_ref, bias_ref, spread_ref, o_ref, acco, accl):
    sb = pl.program_id(1)
    n_qblk = SWA_SUPER // SWA_BLOCK
    stat_pad = jnp.zeros((LANES - SWA_HEADS, SWA_BLOCK), F32)
    last_branch = len(DILATED_BRANCHES) - 1

    for bi, (_, d) in enumerate(DILATED_BRANCHES):
        dsh = d.bit_length() - 1
        per_class = n_qblk // d

        def rows(start, d=d):
            if d > 1:
                return pl.ds(start, SWA_BLOCK, stride=d)
            return pl.ds(pl.multiple_of(start, SWA_BLOCK), SWA_BLOCK)

        def qblock(idx, carry, bi=bi, d=d, dsh=dsh, per_class=per_class, rows=rows):
            r = jnp.bitwise_and(idx, d - 1)
            n = jnp.right_shift(idx, dsh)
            loc = r + n * (d * SWA_BLOCK)
            cur = sb * SWA_SUPER + loc
            first = (sb * per_class + n) == 0
            prev = jnp.maximum(cur - d * SWA_BLOCK, r)
            poff = pl.multiple_of(jnp.where(first, SWA_BLOCK, 0), SWA_BLOCK)
            scores, values = [], []
            for j in range(SWA_SLABS):
                qb = q_ref[j, rows(loc), :].astype(BF16)
                kw = jnp.concatenate([k_ref[j, rows(prev), :], k_ref[j, rows(cur), :]], axis=0).astype(BF16)
                vw = jnp.concatenate([v_ref[j, rows(prev), :], v_ref[j, rows(cur), :]], axis=0).astype(BF16)
                for lo in range(0, LANES, SWA_DH):
                    scores.append(_bdot_nt(kw[:, lo:lo + SWA_DH], qb[:, lo:lo + SWA_DH]))
                    values.append(vw[:, lo:lo + SWA_DH])
            mxs, dens, probs = [], [], []
            for h in range(SWA_HEADS):
                bias = jnp.concatenate([bias_ref[bi, h, pl.ds(poff, SWA_BLOCK), :],
                                        bias_ref[bi, h, 2 * SWA_BLOCK:3 * SWA_BLOCK, :]], axis=0)
                st = scores[h] + bias
                mx = jnp.max(st, axis=0, keepdims=True)
                p = jnp.exp(st - mx)
                mxs.append(mx)
                dens.append(jnp.sum(p, axis=0, keepdims=True))
                probs.append(p.astype(BF16))
            outs = [_bdot_tn(values[h], probs[h]) for h in range(SWA_HEADS)]
            mx = jnp.concatenate(mxs, axis=0)
            den = jnp.concatenate(dens, axis=0)
            lse = mx + jnp.log(den)
            if bi == 0:
                scale = 1.0 / den
                new = lse
            else:
                lacc = jnp.transpose(accl[rows(loc), :])[0:SWA_HEADS]
                m2 = jnp.maximum(lacc, lse)
                ea = jnp.exp(lacc - m2)
                tot = ea + jnp.exp(lse - m2)
                scale = jnp.exp(mx - m2) / tot
                keep = ea / tot
                new = m2 + jnp.log(tot)
            o_tok = jnp.transpose(
                jnp.concatenate([outs[h] * scale[h:h + 1] for h in range(SWA_HEADS)], axis=0))
            if bi > 0:
                kpad = jnp.concatenate([keep, jnp.zeros((STAT_ROWS - SWA_HEADS, SWA_BLOCK), F32)], axis=0)
                hi = kpad.astype(BF16)
                lo2 = (kpad - hi.astype(F32)).astype(BF16)
                spread = spread_ref[...]
                keep_tok = _bdot_tn(hi, spread) + _bdot_tn(lo2, spread)
            for j in range(SWA_SLABS):
                o_slab = o_tok[:, j * LANES:(j + 1) * LANES]
                if bi > 0:
                    o_slab = acco[j, rows(loc), :] * keep_tok[:, j * LANES:(j + 1) * LANES] + o_slab
                acco[j, rows(loc), :] = o_slab
            if bi < last_branch:
                accl[rows(loc), :] = jnp.transpose(jnp.concatenate([new, stat_pad], axis=0))
            return carry

        lax.fori_loop(0, n_qblk, qblock, 0)
    for j in range(SWA_SLABS):
        o_ref[:, j * LANES:(j + 1) * LANES] = acco[j].astype(BF16)


def _dilated_attention(qb, kb, vb, rel_bias, b, s):
    bias = jnp.stack([_branch_bias(rel_bias, w, d) for w, d in DILATED_BRANCHES])
    spread = np.zeros((STAT_ROWS, SWA_W), np.float32)
    for h in range(SWA_HEADS):
        spread[h, h * SWA_DH:(h + 1) * SWA_DH] = 1.0
    once = pl.Buffered(1)
    spans = s // SWA_SUPER
    span = lambda i, j: (0, i * spans + j, 0)
    seq = lambda i, j: (0, i, 0)
    return pl.pallas_call(
        _swa_body,
        grid=(b, spans),
        in_specs=[
            pl.BlockSpec((SWA_SLABS, SWA_SUPER, LANES), span),
            pl.BlockSpec((SWA_SLABS, s, LANES), seq, pipeline_mode=once),
            pl.BlockSpec((SWA_SLABS, s, LANES), seq, pipeline_mode=once),
            pl.BlockSpec(bias.shape, lambda i, j: (0, 0, 0, 0), pipeline_mode=once),
            pl.BlockSpec((STAT_ROWS, SWA_W), lambda i, j: (0, 0)),
        ],
        out_specs=pl.BlockSpec((SWA_SUPER, SWA_W), lambda i, j: (i * spans + j, 0)),
        out_shape=jax.ShapeDtypeStruct((b * s, SWA_W), BF16),
        scratch_shapes=[pltpu.VMEM((SWA_SLABS, SWA_SUPER, LANES), F32), pltpu.VMEM((SWA_SUPER, LANES), F32)],
        compiler_params=_cparams(("parallel", "arbitrary")),
        name="swa",
    )(qb, kb, vb, bias, jnp.asarray(spread, BF16))


TOKEN_TILE = D_MODEL // LANES


def _put_token_tiles(ref, rows):
    m = rows.shape[0]
    for s in range(TOKEN_TILE):
        ref[pl.ds(s, m, stride=TOKEN_TILE), :] = rows[:, s * LANES:(s + 1) * LANES]


def _get_token_tiles(ref, m, base=0):
    return jnp.concatenate([ref[pl.ds(base + s, m, stride=TOKEN_TILE), :] for s in range(TOKEN_TILE)], axis=-1)


def _out_proj_body(x_ref, oa_ref, ob_ref, w_ref, nw_ref, wr_ref, br_ref, h_ref, hn_ref, e_ref, g_ref):
    h = (x_ref[...] + _bdot(oa_ref[...], w_ref[0:GDN_W, :]) + _bdot(ob_ref[...], w_ref[GDN_W:, :]))
    h_ref[...] = h
    hn = h * lax.rsqrt(jnp.mean(h * h, axis=-1, keepdims=True) + EPS) * nw_ref[...]
    _put_token_tiles(hn_ref, hn)
    logits = jnp.dot(hn, wr_ref[...], preferred_element_type=F32, precision=lax.Precision.HIGHEST) + br_ref[...]
    lane = lax.broadcasted_iota(I32, logits.shape, 1)
    e_tile = jnp.zeros(logits.shape, I32)
    g_tile = jnp.zeros(logits.shape, F32)
    top0 = None
    den = None
    for k in range(TOP_K):
        mx = jnp.max(logits, axis=-1, keepdims=True)
        idx = jnp.min(jnp.where(logits == mx, lane, LANES), axis=-1, keepdims=True)
        logits = jnp.where(lane == idx, NEG_INF, logits)
        if k == 0:
            top0 = mx
        ex = jnp.exp(mx - top0)
        den = ex if k == 0 else den + ex
        e_tile = jnp.where(lane == k, idx, e_tile)
        g_tile = jnp.where(lane == k, ex, g_tile)
    e_ref[...] = e_tile
    g_ref[...] = g_tile / den


def _out_proj(x2, oa, ob, w_out, norm2_w, w_router, b_router, tm=512):
    t = x2.shape[0]
    wr = jnp.pad(w_router.astype(F32), ((0, 0), (0, LANES - N_EXPERTS)))
    br = jnp.concatenate([b_router.astype(F32), jnp.full((LANES - N_EXPERTS,), NEG_INF, F32)]).reshape(1, LANES)
    row = lambda i: (i, 0)
    full = lambda i: (0, 0)
    return pl.pallas_call(
        _out_proj_body,
        grid=(t // tm,),
        in_specs=[
            pl.BlockSpec((tm, D_MODEL), row),
            pl.BlockSpec((tm, GDN_W), row),
            pl.BlockSpec((tm, SWA_W), row),
            pl.BlockSpec((GDN_W + SWA_W, D_MODEL), full),
            pl.BlockSpec((1, D_MODEL), full),
            pl.BlockSpec((D_MODEL, LANES), full),
            pl.BlockSpec((1, LANES), full),
        ],
        out_specs=[
            pl.BlockSpec((tm, D_MODEL), row),
            pl.BlockSpec((tm * TOKEN_TILE, LANES), row),
            pl.BlockSpec((tm, LANES), row),
            pl.BlockSpec((tm, LANES), row),
        ],
        out_shape=[
            jax.ShapeDtypeStruct((t, D_MODEL), F32),
            jax.ShapeDtypeStruct((t * TOKEN_TILE, LANES), F32),
            jax.ShapeDtypeStruct((t, LANES), I32),
            jax.ShapeDtypeStruct((t, LANES), F32),
        ],
        compiler_params=_cparams(("parallel",)),
        name="out_proj",
    )(x2, oa, ob, w_out.astype(BF16), norm2_w.astype(F32).reshape(1, D_MODEL), wr, br)


def _moe_plan(top_e):
    t = top_e.shape[0]
    n = t * TOP_K
    p_rows = n + N_EXPERTS * MOE_BLOCK
    nblk = p_rows // MOE_BLOCK
    e_flat = top_e.reshape(-1)
    experts = jnp.arange(N_EXPERTS, dtype=I32)
    counts = jnp.sum((e_flat[:, None] == experts[None, :]).astype(I32), axis=0)
    padded = (counts + MOE_BLOCK - 1) // MOE_BLOCK * MOE_BLOCK
    pad_end = jnp.cumsum(padded)
    cand = jnp.arange(MOE_BLOCK, dtype=I32)[None, :] < (padded - counts)[:, None]
    cand_key = jnp.where(cand, 2 * experts[:, None] + 1, 2 * N_EXPERTS).reshape(-1)
    keys = jnp.concatenate([2 * e_flat, cand_key])
    vals = jnp.concatenate([jnp.arange(n, dtype=I32), jnp.full((N_EXPERTS * MOE_BLOCK,), -1, I32)])
    _, src = lax.sort_key_val(keys, vals)
    is_pad = src < 0
    pad_rank = jnp.cumsum(is_pad.astype(I32)) - 1
    tok = jnp.right_shift(src, TOP_K.bit_length() - 1)
    gather_tok = jnp.where(is_pad, 0, tok)
    scatter_row = jnp.where(is_pad, n + pad_rank, jnp.bitwise_and(src, TOP_K - 1) * t + tok)
    blk_start = jnp.arange(nblk, dtype=I32) * MOE_BLOCK
    blk_expert = jnp.minimum(jnp.sum((blk_start[:, None] >= pad_end[None, :]).astype(I32), axis=1), N_EXPERTS - 1)
    return gather_tok.reshape(nblk, 1, MOE_BLOCK), scatter_row.reshape(nblk, 1, MOE_BLOCK), blk_expert


MOE_COL_CHUNK = 256
MOE_DMA_GROUPS = 2 * D_FF // MOE_COL_CHUNK


def _moe_body(bexp_ref, gidx_ref, sidx_ref, hn_ref, wgu_ref, bgu_ref, wdn_ref, bdn_ref, y_ref,
              xbuf, ybuf, gsem, ssem):
    i = pl.program_id(0)
    last = pl.num_programs(0) - 1
    slot = i % 2

    def gather_copy(tok, s, r):
        return pltpu.make_async_copy(hn_ref.at[tok], xbuf.at[s, pl.ds(r * TOKEN_TILE, TOKEN_TILE)], gsem.at[s])

    def scatter_copy(row, r):
        return pltpu.make_async_copy(ybuf.at[pl.ds(r * TOKEN_TILE, TOKEN_TILE)], y_ref.at[row], ssem)

    @pl.when(i == 0)
    def _():
        for r in range(MOE_BLOCK):
            gather_copy(gidx_ref[0, 0, r], 0, r).start()
        ybuf[...] = jnp.zeros_like(ybuf)

    for r in range(MOE_BLOCK):
        gather_copy(0, slot, r).wait()

    per_group = MOE_BLOCK // MOE_DMA_GROUPS

    def start_row_dmas(group):
        for r in range(group * per_group, (group + 1) * per_group):
            scatter_copy(sidx_ref[0, 0, r], r).start()
            gather_copy(gidx_ref[1, 0, r], 1 - slot, r).start()

    x = _get_token_tiles(xbuf.at[slot], MOE_BLOCK).astype(BF16)
    n_col = D_FF // MOE_COL_CHUNK
    acts = []
    for c in range(n_col):
        start_row_dmas(c)
        lo = c * MOE_COL_CHUNK
        g = jnp.dot(x, wgu_ref[0, :, lo:lo + MOE_COL_CHUNK], preferred_element_type=F32)
        u = jnp.dot(x, wgu_ref[0, :, D_FF + lo:D_FF + lo + MOE_COL_CHUNK], preferred_element_type=F32)
        g = jnp.minimum(g + bgu_ref[0, :, lo:lo + MOE_COL_CHUNK], SWIGLU_LIMIT)
        u = jnp.clip(u + bgu_ref[0, :, D_FF + lo:D_FF + lo + MOE_COL_CHUNK], -SWIGLU_LIMIT, SWIGLU_LIMIT)
        acts.append(((u + 1.0) * g * jax.nn.sigmoid(SWIGLU_ALPHA * g)).astype(BF16))
    act = jnp.concatenate(acts, axis=-1)
    outs = []
    for c in range(D_MODEL // MOE_COL_CHUNK):
        start_row_dmas(n_col + c)
        lo = c * MOE_COL_CHUNK
        outs.append(jnp.dot(act, wdn_ref[0, :, lo:lo + MOE_COL_CHUNK], preferred_element_type=F32)
                    + bdn_ref[0, :, lo:lo + MOE_COL_CHUNK])
    y = jnp.concatenate(outs, axis=-1)

    for r in range(MOE_BLOCK):
        scatter_copy(0, r).wait()
    _put_token_tiles(ybuf, y)

    @pl.when(i == last)
    def _():
        for r in range(MOE_BLOCK):
            scatter_copy(sidx_ref[1, 0, r], r).start()
        for r in range(MOE_BLOCK):
            scatter_copy(0, r).wait()
        for r in range(MOE_BLOCK):
            gather_copy(0, 1 - slot, r).wait()


def _moe_ffn(hn, plan, w_gu, b_gu, w_dn, b_dn):
    t = hn.shape[0] // TOKEN_TILE
    gather_tok, scatter_row, blk_expert = plan
    nblk = gather_tok.shape[0]
    n_rows = t * TOP_K + (N_EXPERTS + 1) * MOE_BLOCK
    gnext = jnp.concatenate([gather_tok[1:], gather_tok[-1:]], axis=0)
    gpair = jnp.concatenate([gather_tok, gnext], axis=1).reshape(nblk * 2, 1, MOE_BLOCK)
    before_first = (n_rows - MOE_BLOCK + jnp.arange(MOE_BLOCK, dtype=I32)).reshape(1, 1, MOE_BLOCK)
    sprev = jnp.concatenate([before_first, scatter_row[:-1]], axis=0)
    spair = jnp.concatenate([sprev, scatter_row], axis=1).reshape(nblk * 2, 1, MOE_BLOCK)
    wsel = lambda i, be: (be[i], 0, 0)
    pair = lambda i, be: (i, 0, 0)
    grid_spec = pltpu.PrefetchScalarGridSpec(
        num_scalar_prefetch=1,
        grid=(nblk,),
        in_specs=[
            pl.BlockSpec((2, 1, MOE_BLOCK), pair, memory_space=pltpu.SMEM),
            pl.BlockSpec((2, 1, MOE_BLOCK), pair, memory_space=pltpu.SMEM),
            pl.BlockSpec(memory_space=pl.ANY),
            pl.BlockSpec((1, D_MODEL, 2 * D_FF), wsel),
            pl.BlockSpec((1, 1, 2 * D_FF), wsel),
            pl.BlockSpec((1, D_FF, D_MODEL), wsel),
            pl.BlockSpec((1, 1, D_MODEL), wsel),
        ],
        out_specs=pl.BlockSpec(memory_space=pl.ANY),
        scratch_shapes=[
            pltpu.VMEM((2, MOE_BLOCK * TOKEN_TILE, LANES), F32),
            pltpu.VMEM((MOE_BLOCK * TOKEN_TILE, LANES), F32),
            pltpu.SemaphoreType.DMA((2,)),
            pltpu.SemaphoreType.DMA(()),
        ],
    )
    return pl.pallas_call(
        _moe_body,
        grid_spec=grid_spec,
        out_shape=jax.ShapeDtypeStruct((n_rows, TOKEN_TILE, LANES), F32),
        compiler_params=_cparams(("arbitrary",)),
        name="moe_ffn",
    )(blk_expert, gpair, spair, hn.reshape(t, TOKEN_TILE, LANES),
      w_gu.astype(BF16), b_gu.astype(F32).reshape(N_EXPERTS, 1, 2 * D_FF),
      w_dn.astype(BF16), b_dn.astype(F32).reshape(N_EXPERTS, 1, D_MODEL))


def _combine_body(h_ref, g_ref, *refs):
    y_refs, o_ref = refs[:TOP_K], refs[TOP_K]
    g = g_ref[...]
    m = g.shape[0]
    moe = g[:, 0:1] * _get_token_tiles(y_refs[0], m)
    for k in range(1, TOP_K):
        moe = moe + g[:, k:k + 1] * _get_token_tiles(y_refs[k], m)
    o_ref[...] = h_ref[...] + moe


def _combine(h, y_slots, gates, tm=512):
    t = h.shape[0]
    y2 = y_slots.reshape(-1, LANES)
    row = lambda i: (i, 0)
    kth = lambda k: (lambda i: (k * (t // tm) + i, 0))
    return pl.pallas_call(
        _combine_body,
        grid=(t // tm,),
        in_specs=[pl.BlockSpec((tm, D_MODEL), row), pl.BlockSpec((tm, LANES), row)]
        + [pl.BlockSpec((tm * TOKEN_TILE, LANES), kth(k)) for k in range(TOP_K)],
        out_specs=pl.BlockSpec((tm, D_MODEL), row),
        out_shape=jax.ShapeDtypeStruct((t, D_MODEL), F32),
        compiler_params=_cparams(("parallel",)),
        name="combine",
    )(h, gates, *([y2] * TOP_K))


def _layer(x, norm1_w, w_in, conv_w, a_log, dt_bias, gdn_norm_w, q_norm_w, k_norm_w, rel_bias,
           w_out, norm2_w, w_router, b_router, w_gate_up, b_gate_up, w_down, b_down):
    b, s, d = x.shape
    t = b * s
    x2 = x.reshape(t, d)
    qkv, gate, ba, qb, kb, vb = _in_proj(x2, norm1_w.astype(F32), _pack_w_in(w_in),
                                         q_norm_w.astype(F32), k_norm_w.astype(F32))
    oa = _gdn(qkv.reshape(b, s, CONV_DIM), gate.reshape(b, s, GDN_W), ba.reshape(b, s, LANES),
              conv_w, a_log, dt_bias, gdn_norm_w)
    ob = _dilated_attention(qb, kb, vb, rel_bias, b, s)
    h, hn, top_e, gates = _out_proj(x2, oa.reshape(t, GDN_W), ob, w_out, norm2_w, w_router, b_router)
    plan = _moe_plan(top_e[:, :TOP_K])
    y_slots = _moe_ffn(hn, plan, w_gate_up, b_gate_up, w_down, b_down)
    return _combine(h, y_slots, gates).reshape(b, s, d)


def kernel(x, norm1_w, w_in, conv_w, a_log, dt_bias, gdn_norm_w, q_norm_w, k_norm_w, rel_bias, w_out, norm2_w,
           w_router, b_router, w_gate_up, b_gate_up, w_down, b_down):
    h = x
    for l in range(norm1_w.shape[0]):
        h = _layer(h, norm1_w[l], w_in[l], conv_w[l], a_log[l], dt_bias[l], gdn_norm_w[l], q_norm_w[l],
                   k_norm_w[l], rel_bias, w_out[l], norm2_w[l], w_router[l], b_router[l], w_gate_up[l],
                   b_gate_up[l], w_down[l], b_down[l])
    return h
```

```python
import functools
import math

import numpy as np
import jax
import jax.numpy as jnp
from jax import lax
from jax.experimental import pallas as pl
from jax.experimental.pallas import tpu as pltpu

F32 = jnp.float32
BF16 = jnp.bfloat16
I32 = jnp.int32

EPS = 1e-6
D_MODEL = 1024
GDN_HEADS = 4
GDN_DK = 128
GDN_DV = 128
CONV_K = 4
SWA_HEADS = 8
SWA_DH = 64
DILATED_BRANCHES = ((128, 1), (512, 4), (2048, 16))
SWA_BLOCK = 128
REL_BUCKETS = 32
REL_MAX_DIST = 2048
N_EXPERTS = 32
TOP_K = 4
D_FF = D_MODEL
SWIGLU_LIMIT = 7.0
SWIGLU_ALPHA = 1.702
MOE_BLOCK = 256

LANES = 128
GDN_W = GDN_HEADS * GDN_DK
SWA_W = SWA_HEADS * SWA_DH
SWA_SLABS = SWA_W // LANES
CONV_DIM = 3 * GDN_W
COL_QKV = 0
COL_GATE = COL_QKV + CONV_DIM
COL_BA = COL_GATE + GDN_W
COL_QB = COL_BA + LANES
COL_KB = COL_QB + SWA_W
COL_VB = COL_KB + SWA_W
IN_PACKED = COL_VB + SWA_W

VMEM_LIMIT = 56 * 1024 * 1024

NEG_INF = float("-inf")


def _cparams(sem, **kw):
    return pltpu.CompilerParams(dimension_semantics=sem, vmem_limit_bytes=VMEM_LIMIT, **kw)


def _bdot(a, b):
    return jnp.dot(a.astype(BF16), b.astype(BF16), preferred_element_type=F32)


def _bdot_nt(a, b):
    return lax.dot_general(a.astype(BF16), b.astype(BF16), (((1,), (1,)), ((), ())),
                           preferred_element_type=F32)


def _bdot_tn(a, b):
    return lax.dot_general(a.astype(BF16), b.astype(BF16), (((0,), (0,)), ((), ())),
                           preferred_element_type=F32)


def _split3(x):
    hi = x.astype(BF16)
    r1 = x - hi.astype(F32)
    mid = r1.astype(BF16)
    lo = (r1 - mid.astype(F32)).astype(BF16)
    return hi, mid, lo


def _dot_split3(a_bf16, x):
    return sum(jnp.dot(a_bf16, part, preferred_element_type=F32) for part in _split3(x))


def _silu(x):
    return x * jax.nn.sigmoid(x)


def _in_proj_body(x_ref, nw_ref, w_ref, qnw_ref, knw_ref, hsum_ref,
                  qkv_ref, gate_ref, ba_ref, qb_ref, kb_ref, vb_ref):
    x = x_ref[...]
    ms = jnp.mean(x * x, axis=-1, keepdims=True)
    xn = (x * lax.rsqrt(ms + EPS) * nw_ref[...]).astype(BF16)

    def seg(lo, width):
        return jnp.dot(xn, w_ref[:, lo:lo + width], preferred_element_type=F32)

    qkv_ref[...] = seg(COL_QKV, CONV_DIM)
    gate_ref[...] = seg(COL_GATE, GDN_W)
    ba_ref[...] = seg(COL_BA, LANES)

    def head_rmsnorm(z, w_row, scale):
        sq = z * z
        hi = sq.astype(BF16)
        lo = (sq - hi.astype(F32)).astype(BF16)
        hmean = (jnp.dot(hi, hsum_ref[...], preferred_element_type=F32)
                 + jnp.dot(lo, hsum_ref[...], preferred_element_type=F32))
        return z * lax.rsqrt(hmean + EPS) * w_row * scale

    def put_slabs(ref, z):
        for j in range(SWA_SLABS):
            ref[j] = z[:, j * LANES:(j + 1) * LANES]

    put_slabs(qb_ref, head_rmsnorm(seg(COL_QB, SWA_W), qnw_ref[...], SWA_DH ** -0.5))
    put_slabs(kb_ref, head_rmsnorm(seg(COL_KB, SWA_W), knw_ref[...], 1.0))
    put_slabs(vb_ref, seg(COL_VB, SWA_W))


def _in_proj(x2, norm1_w, w_packed, q_norm_w, k_norm_w, tm=512):
    t = x2.shape[0]
    hsum = jnp.asarray(np.kron(np.eye(SWA_HEADS), np.ones((SWA_DH, SWA_DH))) / SWA_DH, BF16)
    row = lambda i: (i, 0)
    full = lambda i: (0, 0)
    slab = lambda i: (0, i, 0)
    return pl.pallas_call(
        _in_proj_body,
        grid=(t // tm,),
        in_specs=[
            pl.BlockSpec((tm, D_MODEL), row),
            pl.BlockSpec((1, D_MODEL), full),
            pl.BlockSpec((D_MODEL, IN_PACKED), full),
            pl.BlockSpec((1, SWA_W), full),
            pl.BlockSpec((1, SWA_W), full),
            pl.BlockSpec((SWA_W, SWA_W), full),
        ],
        out_specs=[
            pl.BlockSpec((tm, CONV_DIM), row),
            pl.BlockSpec((tm, GDN_W), row),
            pl.BlockSpec((tm, LANES), row),
            pl.BlockSpec((SWA_SLABS, tm, LANES), slab),
            pl.BlockSpec((SWA_SLABS, tm, LANES), slab),
            pl.BlockSpec((SWA_SLABS, tm, LANES), slab),
        ],
        out_shape=[
            jax.ShapeDtypeStruct((t, CONV_DIM), F32),
            jax.ShapeDtypeStruct((t, GDN_W), F32),
            jax.ShapeDtypeStruct((t, LANES), F32),
            jax.ShapeDtypeStruct((SWA_SLABS, t, LANES), F32),
            jax.ShapeDtypeStruct((SWA_SLABS, t, LANES), F32),
            jax.ShapeDtypeStruct((SWA_SLABS, t, LANES), F32),
        ],
        compiler_params=_cparams(("parallel",)),
        name="in_proj",
    )(x2, norm1_w.reshape(1, D_MODEL), w_packed,
      jnp.tile(q_norm_w, SWA_HEADS).reshape(1, SWA_W), jnp.tile(k_norm_w, SWA_HEADS).reshape(1, SWA_W), hsum)


def _pack_w_in(w_in):
    o = np.cumsum([0, GDN_W, GDN_W, GDN_W, GDN_W, GDN_HEADS, GDN_HEADS, SWA_W, SWA_W, SWA_W])
    ba = jnp.pad(w_in[:, o[4]:o[6]], ((0, 0), (0, LANES - 2 * GDN_HEADS)))
    return jnp.concatenate([w_in[:, o[0]:o[4]], ba, w_in[:, o[6]:o[9]]], axis=1).astype(BF16)


def _unit_lower_inverses(a_list, n, blk):
    ii = lax.broadcasted_iota(I32, (n, n), 0)
    jj = lax.broadcasted_iota(I32, (n, n), 1)
    same = lambda sh: jnp.right_shift(ii, sh) == jnp.right_shift(jj, sh)
    eye = (ii == jj).astype(F32)
    pair = same(1)
    t_list = [eye - jnp.where(pair, a, 0.0) for a in a_list]
    sh = 1
    while (1 << sh) < blk:
        join = same(sh + 1) & jnp.logical_not(same(sh))
        te = [_bdot(t, jnp.where(join, a, 0.0)) for t, a in zip(t_list, a_list)]
        tet = [_bdot(x, t) for x, t in zip(te, t_list)]
        t_list = [t - y for t, y in zip(t_list, tet)]
        sh += 1
    return t_list


def _gdn_body(chunk, n_chunks, qkv_ref, gate_ref, ba_ref, convw_ref, alane_ref, dtlane_ref, normw_ref,
              o_ref, tail_ref, state_ref):
    c = chunk
    rows = chunk * n_chunks

    @pl.when(pl.program_id(1) == 0)
    def _():
        tail_ref[...] = jnp.zeros_like(tail_ref)
        state_ref[...] = jnp.zeros_like(state_ref)

    x = qkv_ref[0]
    xs = jnp.concatenate([tail_ref[...], x], axis=0)
    w = convw_ref[...]
    conv = (xs[5:5 + rows] * w[0:1] + xs[6:6 + rows] * w[1:2] + xs[7:7 + rows] * w[2:3] + x * w[3:4])
    tail_ref[...] = x[rows - 8:rows]
    y = _silu(conv)

    ba = ba_ref[0]
    beta_all = jax.nn.sigmoid(ba)
    z = ba + dtlane_ref[...]
    softplus = jnp.maximum(z, 0.0) + jnp.log(1.0 + jnp.exp(-jnp.abs(z)))
    g_all = -alane_ref[...] * softplus
    ri = lax.broadcasted_iota(I32, (rows, rows), 0)
    ci = lax.broadcasted_iota(I32, (rows, rows), 1)
    csh = c.bit_length() - 1
    tri = ((ri >= ci) & (jnp.right_shift(ri, csh) == jnp.right_shift(ci, csh))).astype(F32)
    gc_all = _dot_split3(tri.astype(BF16), g_all)

    heads = range(GDN_HEADS)
    qn, kn, vv = [], [], []
    for h in heads:
        lo = h * GDN_DK
        q = y[:, lo:lo + GDN_DK]
        k = y[:, GDN_W + lo:GDN_W + lo + GDN_DK]
        qn.append(q * lax.rsqrt(jnp.sum(q * q, axis=-1, keepdims=True) + EPS) * (GDN_DK ** -0.5))
        kn.append(k * lax.rsqrt(jnp.sum(k * k, axis=-1, keepdims=True) + EPS))
        vv.append(y[:, 2 * GDN_W + lo:2 * GDN_W + lo + GDN_DV])

    hs = GDN_HEADS * c
    hsh = csh
    ii = lax.broadcasted_iota(I32, (hs, hs), 0)
    jj = lax.broadcasted_iota(I32, (hs, hs), 1)
    same_head = jnp.right_shift(ii, hsh) == jnp.right_shift(jj, hsh)
    row_head = jnp.right_shift(lax.broadcasted_iota(I32, (hs, GDN_DV), 0), hsh)

    def stack(parts, r0):
        return jnp.concatenate([p[r0:r0 + c] for p in parts], axis=0)

    def col(src, lane0, r0):
        return jnp.concatenate([src[r0:r0 + c, lane0 + h:lane0 + h + 1] for h in heads], axis=0)

    def diag_blocks(p):
        out = jnp.where(row_head == 0, p[:, 0:GDN_DV], 0.0)
        for h in range(1, GDN_HEADS):
            out = out + jnp.where(row_head == h, p[:, h * GDN_DV:(h + 1) * GDN_DV], 0.0)
        return out

    q_st, u_rhs, kdec, cdecay, a_kk, a_qk = ([] for _ in range(6))
    for n in range(n_chunks):
        r0 = n * c
        k = stack(kn, r0)
        q = stack(qn, r0)
        v = stack(vv, r0)
        beta = col(beta_all, 0, r0)
        gcol = col(gc_all, GDN_HEADS, r0)
        grow = jnp.transpose(jnp.broadcast_to(gcol, (hs, LANES)))[0:1, :]
        glast = jnp.concatenate(
            [jnp.broadcast_to(gc_all[r0 + c - 1:r0 + c, GDN_HEADS + h:GDN_HEADS + h + 1], (c, 1)) for h in heads],
            axis=0)
        dec = jnp.exp(jnp.where(same_head & (ii >= jj), gcol - grow, NEG_INF))
        eg = jnp.exp(gcol)
        kb = k * beta
        prod = _bdot_nt(jnp.concatenate([kb, q], axis=0), k)
        a_kk.append(jnp.where(ii > jj, prod[:hs] * dec, 0.0))
        a_qk.append(prod[hs:] * dec)
        q_st.append(q * eg)
        u_rhs.append(jnp.concatenate([v * beta, kb * eg], axis=-1))
        kdec.append(k * jnp.exp(glast - gcol))
        cdecay.append(jnp.concatenate(
            [jnp.broadcast_to(jnp.exp(gc_all[r0 + c - 1:r0 + c, GDN_HEADS + h:GDN_HEADS + h + 1]), (1, GDN_DV))
             for h in heads], axis=-1))
    t_inv = _unit_lower_inverses(a_kk, hs, c)
    sols = [_bdot(t, r) for t, r in zip(t_inv, u_rhs)]

    gate = gate_ref[0]
    normw = normw_ref[...]
    s_cat = state_ref[...]
    for n in range(n_chunks):
        r0 = n * c
        u, wk = sols[n][:, :GDN_DV], sols[n][:, GDN_DV:]
        p = _bdot(jnp.concatenate([wk, q_st[n]], axis=0), s_cat)
        v_new = u - diag_blocks(p[:hs])
        o = diag_blocks(p[hs:]) + _bdot(a_qk[n], v_new)
        v_bd = jnp.concatenate([jnp.where(row_head == h, v_new, 0.0) for h in heads], axis=-1)
        s_cat = s_cat * cdecay[n] + _bdot_tn(kdec[n], v_bd)
        o = o * lax.rsqrt(jnp.mean(o * o, axis=-1, keepdims=True) + EPS) * normw
        for h in heads:
            lo = h * GDN_DV
            o_ref[0, r0:r0 + c, lo:lo + GDN_DV] = o[h * c:(h + 1) * c] * _silu(gate[r0:r0 + c, lo:lo + GDN_DV])
    state_ref[...] = s_cat


def _gdn(qkv, gate, ba, conv_w, a_log, dt_bias, gdn_norm_w, chunk=64, n_chunks=4):
    b, s, _ = qkv.shape
    rows = chunk * n_chunks
    pad = jnp.zeros((LANES - 2 * GDN_HEADS,), F32)
    alane = jnp.concatenate([jnp.zeros((GDN_HEADS,), F32), jnp.exp(a_log.astype(F32)), pad]).reshape(1, LANES)
    dtlane = jnp.concatenate([jnp.zeros((GDN_HEADS,), F32), dt_bias.astype(F32), pad]).reshape(1, LANES)
    blk = lambda i, j: (i, j, 0)
    full = lambda i, j: (0, 0)
    return pl.pallas_call(
        functools.partial(_gdn_body, chunk, n_chunks),
        grid=(b, s // rows),
        in_specs=[
            pl.BlockSpec((1, rows, CONV_DIM), blk),
            pl.BlockSpec((1, rows, GDN_W), blk),
            pl.BlockSpec((1, rows, LANES), blk),
            pl.BlockSpec((CONV_K, CONV_DIM), full),
            pl.BlockSpec((1, LANES), full),
            pl.BlockSpec((1, LANES), full),
            pl.BlockSpec((1, GDN_DV), full),
        ],
        out_specs=pl.BlockSpec((1, rows, GDN_W), blk),
        out_shape=jax.ShapeDtypeStruct((b, s, GDN_W), F32),
        scratch_shapes=[pltpu.VMEM((8, CONV_DIM), F32), pltpu.VMEM((GDN_DK, GDN_HEADS * GDN_DV), F32)],
        compiler_params=_cparams(("parallel", "arbitrary")),
        name="gdn",
    )(qkv, gate, ba, conv_w.astype(F32), alane, dtlane, gdn_norm_w.astype(F32).reshape(1, GDN_DV))


def _t5_bucket(dist):
    max_exact = REL_BUCKETS // 2
    n = np.maximum(dist, 0)
    large = max_exact + (np.log(np.maximum(n, 1) / max_exact) / math.log(REL_MAX_DIST / max_exact)
                         * (REL_BUCKETS - max_exact)).astype(np.int32)
    large = np.minimum(large, REL_BUCKETS - 1)
    return np.where(n < max_exact, n, large).astype(np.int32)


def _branch_bias(rel_bias, window, dilation):
    kj = np.arange(2 * SWA_BLOCK)[:, None]
    qi = np.arange(SWA_BLOCK)[None, :]
    steps = qi + SWA_BLOCK - kj
    valid = (steps >= 0) & (steps <= window // dilation)
    onehot = np.eye(REL_BUCKETS, dtype=np.float32)[_t5_bucket(steps * dilation).reshape(-1)]
    bias = jnp.dot(jnp.asarray(onehot), rel_bias.astype(F32), precision=lax.Precision.HIGHEST)
    bias = bias.reshape(2 * SWA_BLOCK, SWA_BLOCK, SWA_HEADS).transpose(2, 0, 1)
    bias = jnp.where(jnp.asarray(valid)[None], bias, NEG_INF)
    return jnp.concatenate([bias[:, :SWA_BLOCK], jnp.full((SWA_HEADS, SWA_BLOCK, SWA_BLOCK), NEG_INF, F32),
                            bias[:, SWA_BLOCK:]], axis=1)


SWA_SUPER = 2048
STAT_ROWS = 16


def _swa_body(q_ref, k_ref, v_ref, bias_ref, spread_ref, o_ref, acco, accl):
    sb = pl.program_id(1)
    n_qblk = SWA_SUPER // SWA_BLOCK
    stat_pad = jnp.zeros((LANES - SWA_HEADS, SWA_BLOCK), F32)
    last_branch = len(DILATED_BRANCHES) - 1

    for bi, (_, d) in enumerate(DILATED_BRANCHES):
        dsh = d.bit_length() - 1
        per_class = n_qblk // d

        def rows(start, d=d):
            if d > 1:
                return pl.ds(start, SWA_BLOCK, stride=d)
            return pl.ds(pl.multiple_of(start, SWA_BLOCK), SWA_BLOCK)

        def qblock(idx, carry, bi=bi, d=d, dsh=dsh, per_class=per_class, rows=rows):
            r = jnp.bitwise_and(idx, d - 1)
            n = jnp.right_shift(idx, dsh)
            loc = r + n * (d * SWA_BLOCK)
            cur = sb * SWA_SUPER + loc
            first = (sb * per_class + n) == 0
            prev = jnp.maximum(cur - d * SWA_BLOCK, r)
            poff = pl.multiple_of(jnp.where(first, SWA_BLOCK, 0), SWA_BLOCK)
            scores, values = [], []
            for j in range(SWA_SLABS):
                qb = q_ref[j, rows(loc), :].astype(BF16)
                kw = jnp.concatenate([k_ref[j, rows(prev), :], k_ref[j, rows(cur), :]], axis=0).astype(BF16)
                vw = jnp.concatenate([v_ref[j, rows(prev), :], v_ref[j, rows(cur), :]], axis=0).astype(BF16)
                for lo in range(0, LANES, SWA_DH):
                    scores.append(_bdot_nt(kw[:, lo:lo + SWA_DH], qb[:, lo:lo + SWA_DH]))
                    values.append(vw[:, lo:lo + SWA_DH])
            mxs, dens, probs = [], [], []
            for h in range(SWA_HEADS):
                bias = jnp.concatenate([bias_ref[bi, h, pl.ds(poff, SWA_BLOCK), :],
                                        bias_ref[bi, h, 2 * SWA_BLOCK:3 * SWA_BLOCK, :]], axis=0)
                st = scores[h] + bias
                mx = jnp.max(st, axis=0, keepdims=True)
                p = jnp.exp(st - mx)
                mxs.append(mx)
                dens.append(jnp.sum(p, axis=0, keepdims=True))
                probs.append(p.astype(BF16))
            outs = [_bdot_tn(values[h], probs[h]) for h in range(SWA_HEADS)]
            mx = jnp.concatenate(mxs, axis=0)
            den = jnp.concatenate(dens, axis=0)
            lse = mx + jnp.log(den)
            if bi == 0:
                scale = 1.0 / den
                new = lse
            else:
                lacc = jnp.transpose(accl[rows(loc), :])[0:SWA_HEADS]
                m2 = jnp.maximum(lacc, lse)
                ea = jnp.exp(lacc - m2)
                tot = ea + jnp.exp(lse - m2)
                scale = jnp.exp(mx - m2) / tot
                keep = ea / tot
                new = m2 + jnp.log(tot)
            o_tok = jnp.transpose(
                jnp.concatenate([outs[h] * scale[h:h + 1] for h in range(SWA_HEADS)], axis=0))
            if bi > 0:
                kpad = jnp.concatenate([keep, jnp.zeros((STAT_ROWS - SWA_HEADS, SWA_BLOCK), F32)], axis=0)
                hi = kpad.astype(BF16)
                lo2 = (kpad - hi.astype(F32)).astype(BF16)
                spread = spread_ref[...]
                keep_tok = _bdot_tn(hi, spread) + _bdot_tn(lo2, spread)
            for j in range(SWA_SLABS):
                o_slab = o_tok[:, j * LANES:(j + 1) * LANES]
                if bi > 0:
                    o_slab = acco[j, rows(loc), :] * keep_tok[:, j * LANES:(j + 1) * LANES] + o_slab
                acco[j, rows(loc), :] = o_slab
            if bi < last_branch:
                accl[rows(loc), :] = jnp.transpose(jnp.concatenate([new, stat_pad], axis=0))
            return carry

        lax.fori_loop(0, n_qblk, qblock, 0)
    for j in range(SWA_SLABS):
        o_ref[:, j * LANES:(j + 1) * LANES] = acco[j].astype(BF16)


def _dilated_attention(qb, kb, vb, rel_bias, b, s):
    bias = jnp.stack([_branch_bias(rel_bias, w, d) for w, d in DILATED_BRANCHES])
    spread = np.zeros((STAT_ROWS, SWA_W), np.float32)
    for h in range(SWA_HEADS):
        spread[h, h * SWA_DH:(h + 1) * SWA_DH] = 1.0
    once = pl.Buffered(1)
    spans = s // SWA_SUPER
    span = lambda i, j: (0, i * spans + j, 0)
    seq = lambda i, j: (0, i, 0)
    return pl.pallas_call(
        _swa_body,
        grid=(b, spans),
        in_specs=[
            pl.BlockSpec((SWA_SLABS, SWA_SUPER, LANES), span),
            pl.BlockSpec((SWA_SLABS, s, LANES), seq, pipeline_mode=once),
            pl.BlockSpec((SWA_SLABS, s, LANES), seq, pipeline_mode=once),
            pl.BlockSpec(bias.shape, lambda i, j: (0, 0, 0, 0), pipeline_mode=once),
            pl.BlockSpec((STAT_ROWS, SWA_W), lambda i, j: (0, 0)),
        ],
        out_specs=pl.BlockSpec((SWA_SUPER, SWA_W), lambda i, j: (i * spans + j, 0)),
        out_shape=jax.ShapeDtypeStruct((b * s, SWA_W), BF16),
        scratch_shapes=[pltpu.VMEM((SWA_SLABS, SWA_SUPER, LANES), F32), pltpu.VMEM((SWA_SUPER, LANES), F32)],
        compiler_params=_cparams(("parallel", "arbitrary")),
        name="swa",
    )(qb, kb, vb, bias, jnp.asarray(spread, BF16))


TOKEN_TILE = D_MODEL // LANES


def _put_token_tiles(ref, rows):
    m = rows.shape[0]
    for s in range(TOKEN_TILE):
        ref[pl.ds(s, m, stride=TOKEN_TILE), :] = rows[:, s * LANES:(s + 1) * LANES]


def _get_token_tiles(ref, m, base=0):
    return jnp.concatenate([ref[pl.ds(base + s, m, stride=TOKEN_TILE), :] for s in range(TOKEN_TILE)], axis=-1)


def _out_proj_body(x_ref, oa_ref, ob_ref, w_ref, nw_ref, wr_ref, br_ref, h_ref, hn_ref, e_ref, g_ref):
    h = (x_ref[...] + _bdot(oa_ref[...], w_ref[0:GDN_W, :]) + _bdot(ob_ref[...], w_ref[GDN_W:, :]))
    h_ref[...] = h
    hn = h * lax.rsqrt(jnp.mean(h * h, axis=-1, keepdims=True) + EPS) * nw_ref[...]
    _put_token_tiles(hn_ref, hn)
    hn_hi = hn.astype(BF16)
    hn_lo = (hn - hn_hi.astype(F32)).astype(BF16)
    logits = (jnp.dot(hn_hi, wr_ref[0], preferred_element_type=F32)
              + jnp.dot(hn_hi, wr_ref[1], preferred_element_type=F32)
              + jnp.dot(hn_lo, wr_ref[0], preferred_element_type=F32) + br_ref[...])
    lane = lax.broadcasted_iota(I32, logits.shape, 1)
    e_tile = jnp.zeros(logits.shape, I32)
    g_tile = jnp.zeros(logits.shape, F32)
    top0 = None
    den = None
    for k in range(TOP_K):
        mx = jnp.max(logits, axis=-1, keepdims=True)
        idx = jnp.min(jnp.where(logits == mx, lane, LANES), axis=-1, keepdims=True)
        logits = jnp.where(lane == idx, NEG_INF, logits)
        if k == 0:
            top0 = mx
        ex = jnp.exp(mx - top0)
        den = ex if k == 0 else den + ex
        e_tile = jnp.where(lane == k, idx, e_tile)
        g_tile = jnp.where(lane == k, ex, g_tile)
    e_ref[...] = e_tile
    g_ref[...] = g_tile / den


def _out_proj(x2, oa, ob, w_out, norm2_w, w_router, b_router, tm=512):
    t = x2.shape[0]
    wr = jnp.pad(w_router.astype(F32), ((0, 0), (0, LANES - N_EXPERTS)))
    wr_hi = wr.astype(BF16)
    wr = jnp.stack([wr_hi, (wr - wr_hi.astype(F32)).astype(BF16)])
    br = jnp.concatenate([b_router.astype(F32), jnp.full((LANES - N_EXPERTS,), NEG_INF, F32)]).reshape(1, LANES)
    row = lambda i: (i, 0)
    full = lambda i: (0, 0)
    return pl.pallas_call(
        _out_proj_body,
        grid=(t // tm,),
        in_specs=[
            pl.BlockSpec((tm, D_MODEL), row),
            pl.BlockSpec((tm, GDN_W), row),
            pl.BlockSpec((tm, SWA_W), row),
            pl.BlockSpec((GDN_W + SWA_W, D_MODEL), full),
            pl.BlockSpec((1, D_MODEL), full),
            pl.BlockSpec((2, D_MODEL, LANES), lambda i: (0, 0, 0)),
            pl.BlockSpec((1, LANES), full),
        ],
        out_specs=[
            pl.BlockSpec((tm, D_MODEL), row),
            pl.BlockSpec((tm * TOKEN_TILE, LANES), row),
            pl.BlockSpec((tm, LANES), row),
            pl.BlockSpec((tm, LANES), row),
        ],
        out_shape=[
            jax.ShapeDtypeStruct((t, D_MODEL), F32),
            jax.ShapeDtypeStruct((t * TOKEN_TILE, LANES), F32),
            jax.ShapeDtypeStruct((t, LANES), I32),
            jax.ShapeDtypeStruct((t, LANES), F32),
        ],
        compiler_params=_cparams(("parallel",)),
        name="out_proj",
    )(x2, oa, ob, w_out.astype(BF16), norm2_w.astype(F32).reshape(1, D_MODEL), wr, br)


def _moe_plan(top_e):
    t = top_e.shape[0]
    n = t * TOP_K
    p_rows = n + N_EXPERTS * MOE_BLOCK
    nblk = p_rows // MOE_BLOCK
    e_flat = top_e.reshape(-1)
    experts = jnp.arange(N_EXPERTS, dtype=I32)
    counts = jnp.sum((e_flat[:, None] == experts[None, :]).astype(I32), axis=0)
    padded = (counts + MOE_BLOCK - 1) // MOE_BLOCK * MOE_BLOCK
    pad_end = jnp.cumsum(padded)
    cand = jnp.arange(MOE_BLOCK, dtype=I32)[None, :] < (padded - counts)[:, None]
    cand_key = jnp.where(cand, 2 * experts[:, None] + 1, 2 * N_EXPERTS).reshape(-1)
    keys = jnp.concatenate([2 * e_flat, cand_key])
    vals = jnp.concatenate([jnp.arange(n, dtype=I32), jnp.full((N_EXPERTS * MOE_BLOCK,), -1, I32)])
    _, src = lax.sort_key_val(keys, vals)
    is_pad = src < 0
    pad_rank = jnp.cumsum(is_pad.astype(I32)) - 1
    tok = jnp.right_shift(src, TOP_K.bit_length() - 1)
    gather_tok = jnp.where(is_pad, 0, tok)
    scatter_row = jnp.where(is_pad, n + pad_rank, jnp.bitwise_and(src, TOP_K - 1) * t + tok)
    blk_start = jnp.arange(nblk, dtype=I32) * MOE_BLOCK
    blk_expert = jnp.minimum(jnp.sum((blk_start[:, None] >= pad_end[None, :]).astype(I32), axis=1), N_EXPERTS - 1)
    return gather_tok.reshape(nblk, 1, MOE_BLOCK), scatter_row.reshape(nblk, 1, MOE_BLOCK), blk_expert


MOE_COL_CHUNK = 256


def _moe_body(bexp_ref, gidx_ref, sidx_ref, hn_ref, wgu_ref, bgu_ref, wdn_ref, bdn_ref, y_ref,
              xbuf, ybuf, wgu_bf, wdn_bf, gsem, ssem):
    i = pl.program_id(0)
    last = pl.num_programs(0) - 1
    slot = i % 2

    def gather_copy(tok, s, r):
        return pltpu.make_async_copy(hn_ref.at[tok], xbuf.at[s, pl.ds(r * TOKEN_TILE, TOKEN_TILE)], gsem.at[s])

    def scatter_copy(row, r):
        return pltpu.make_async_copy(ybuf.at[pl.ds(r * TOKEN_TILE, TOKEN_TILE)], y_ref.at[row], ssem)

    @pl.when(i == 0)
    def _():
        for r in range(MOE_BLOCK):
            gather_copy(gidx_ref[0, 0, r], 0, r).start()
        ybuf[...] = jnp.zeros_like(ybuf)

    @pl.when(i >= 0)
    def _():
        for r in range(MOE_BLOCK):
            scatter_copy(sidx_ref[0, 0, r], r).start()
            gather_copy(gidx_ref[1, 0, r], 1 - slot, r).start()

    @pl.when(jnp.logical_or(i == 0, bexp_ref[i] != bexp_ref[jnp.maximum(i - 1, 0)]))
    def _():
        wgu_bf[...] = wgu_ref[0].astype(BF16)
        wdn_bf[...] = wdn_ref[0].astype(BF16)

    for r in range(MOE_BLOCK):
        gather_copy(0, slot, r).wait()
    x = _get_token_tiles(xbuf.at[slot], MOE_BLOCK).astype(BF16)
    n_col = D_FF // MOE_COL_CHUNK
    acts = []
    for c in range(n_col):
        lo = c * MOE_COL_CHUNK
        g = jnp.dot(x, wgu_bf[:, lo:lo + MOE_COL_CHUNK], preferred_element_type=F32)
        u = jnp.dot(x, wgu_bf[:, D_FF + lo:D_FF + lo + MOE_COL_CHUNK], preferred_element_type=F32)
        g = jnp.minimum(g + bgu_ref[0, :, lo:lo + MOE_COL_CHUNK], SWIGLU_LIMIT)
        u = jnp.clip(u + bgu_ref[0, :, D_FF + lo:D_FF + lo + MOE_COL_CHUNK], -SWIGLU_LIMIT, SWIGLU_LIMIT)
        acts.append(((u + 1.0) * g * jax.nn.sigmoid(SWIGLU_ALPHA * g)).astype(BF16))
    act = jnp.concatenate(acts, axis=-1)
    outs = []
    for c in range(D_MODEL // MOE_COL_CHUNK):
        lo = c * MOE_COL_CHUNK
        outs.append(jnp.dot(act, wdn_bf[:, lo:lo + MOE_COL_CHUNK], preferred_element_type=F32)
                    + bdn_ref[0, :, lo:lo + MOE_COL_CHUNK])
    y = jnp.concatenate(outs, axis=-1)

    for r in range(MOE_BLOCK):
        scatter_copy(0, r).wait()
    _put_token_tiles(ybuf, y)

    @pl.when(i == last)
    def _():
        for r in range(MOE_BLOCK):
            scatter_copy(sidx_ref[1, 0, r], r).start()
        for r in range(MOE_BLOCK):
            scatter_copy(0, r).wait()
        for r in range(MOE_BLOCK):
            gather_copy(0, 1 - slot, r).wait()


def _moe_ffn(hn, plan, w_gu, b_gu, w_dn, b_dn):
    t = hn.shape[0] // TOKEN_TILE
    gather_tok, scatter_row, blk_expert = plan
    nblk = gather_tok.shape[0]
    n_rows = t * TOP_K + (N_EXPERTS + 1) * MOE_BLOCK
    gnext = jnp.concatenate([gather_tok[1:], gather_tok[-1:]], axis=0)
    gpair = jnp.concatenate([gather_tok, gnext], axis=1).reshape(nblk * 2, 1, MOE_BLOCK)
    before_first = (n_rows - MOE_BLOCK + jnp.arange(MOE_BLOCK, dtype=I32)).reshape(1, 1, MOE_BLOCK)
    sprev = jnp.concatenate([before_first, scatter_row[:-1]], axis=0)
    spair = jnp.concatenate([sprev, scatter_row], axis=1).reshape(nblk * 2, 1, MOE_BLOCK)
    wsel = lambda i, be: (be[i], 0, 0)
    pair = lambda i, be: (i, 0, 0)
    grid_spec = pltpu.PrefetchScalarGridSpec(
        num_scalar_prefetch=1,
        grid=(nblk,),
        in_specs=[
            pl.BlockSpec((2, 1, MOE_BLOCK), pair, memory_space=pltpu.SMEM),
            pl.BlockSpec((2, 1, MOE_BLOCK), pair, memory_space=pltpu.SMEM),
            pl.BlockSpec(memory_space=pl.ANY),
            pl.BlockSpec((1, D_MODEL, 2 * D_FF), wsel),
            pl.BlockSpec((1, 1, 2 * D_FF), wsel),
            pl.BlockSpec((1, D_FF, D_MODEL), wsel),
            pl.BlockSpec((1, 1, D_MODEL), wsel),
        ],
        out_specs=pl.BlockSpec(memory_space=pl.ANY),
        scratch_shapes=[
            pltpu.VMEM((2, MOE_BLOCK * TOKEN_TILE, LANES), F32),
            pltpu.VMEM((MOE_BLOCK * TOKEN_TILE, LANES), F32),
            pltpu.VMEM((D_MODEL, 2 * D_FF), BF16),
            pltpu.VMEM((D_FF, D_MODEL), BF16),
            pltpu.SemaphoreType.DMA((2,)),
            pltpu.SemaphoreType.DMA(()),
        ],
    )
    return pl.pallas_call(
        _moe_body,
        grid_spec=grid_spec,
        out_shape=jax.ShapeDtypeStruct((n_rows, TOKEN_TILE, LANES), F32),
        compiler_params=_cparams(("arbitrary",)),
        name="moe_ffn",
    )(blk_expert, gpair, spair, hn.reshape(t, TOKEN_TILE, LANES),
      w_gu.astype(F32), b_gu.astype(F32).reshape(N_EXPERTS, 1, 2 * D_FF),
      w_dn.astype(F32), b_dn.astype(F32).reshape(N_EXPERTS, 1, D_MODEL))


def _combine_body(h_ref, g_ref, *refs):
    y_refs, o_ref = refs[:TOP_K], refs[TOP_K]
    g = g_ref[...]
    m = g.shape[0]
    moe = g[:, 0:1] * _get_token_tiles(y_refs[0], m)
    for k in range(1, TOP_K):
        moe = moe + g[:, k:k + 1] * _get_token_tiles(y_refs[k], m)
    o_ref[...] = h_ref[...] + moe


def _combine(h, y_slots, gates, tm=512):
    t = h.shape[0]
    y2 = y_slots.reshape(-1, LANES)
    row = lambda i: (i, 0)
    kth = lambda k: (lambda i: (k * (t // tm) + i, 0))
    return pl.pallas_call(
        _combine_body,
        grid=(t // tm,),
        in_specs=[pl.BlockSpec((tm, D_MODEL), row), pl.BlockSpec((tm, LANES), row)]
        + [pl.BlockSpec((tm * TOKEN_TILE, LANES), kth(k)) for k in range(TOP_K)],
        out_specs=pl.BlockSpec((tm, D_MODEL), row),
        out_shape=jax.ShapeDtypeStruct((t, D_MODEL), F32),
        compiler_params=_cparams(("parallel",)),
        name="combine",
    )(h, gates, *([y2] * TOP_K))


def _layer(x, norm1_w, w_in, conv_w, a_log, dt_bias, gdn_norm_w, q_norm_w, k_norm_w, rel_bias,
           w_out, norm2_w, w_router, b_router, w_gate_up, b_gate_up, w_down, b_down):
    b, s, d = x.shape
    t = b * s
    x2 = x.reshape(t, d)
    qkv, gate, ba, qb, kb, vb = _in_proj(x2, norm1_w.astype(F32), _pack_w_in(w_in),
                                         q_norm_w.astype(F32), k_norm_w.astype(F32))
    oa = _gdn(qkv.reshape(b, s, CONV_DIM), gate.reshape(b, s, GDN_W), ba.reshape(b, s, LANES),
              conv_w, a_log, dt_bias, gdn_norm_w)
    ob = _dilated_attention(qb, kb, vb, rel_bias, b, s)
    h, hn, top_e, gates = _out_proj(x2, oa.reshape(t, GDN_W), ob, w_out, norm2_w, w_router, b_router)
    plan = _moe_plan(top_e[:, :TOP_K])
    y_slots = _moe_ffn(hn, plan, w_gate_up, b_gate_up, w_down, b_down)
    return _combine(h, y_slots, gates).reshape(b, s, d)


def kernel(x, norm1_w, w_in, conv_w, a_log, dt_bias, gdn_norm_w, q_norm_w, k_norm_w, rel_bias, w_out, norm2_w,
           w_router, b_router, w_gate_up, b_gate_up, w_down, b_down):
    h = x
    for l in range(norm1_w.shape[0]):
        h = _layer(h, norm1_w[l], w_in[l], conv_w[l], a_log[l], dt_bias[l], gdn_norm_w[l], q_norm_w[l],
                   k_norm_w[l], rel_bias, w_out[l], norm2_w[l], w_router[l], b_router[l], w_gate_up[l],
                   b_gate_up[l], w_down[l], b_down[l])
    return h
```

```python
import functools
import math

import numpy as np
import jax
import jax.numpy as jnp
from jax import lax
from jax.experimental import pallas as pl
from jax.experimental.pallas import tpu as pltpu

F32 = jnp.float32
BF16 = jnp.bfloat16
I32 = jnp.int32

EPS = 1e-6
D_MODEL = 1024
GDN_HEADS = 4
GDN_DK = 128
GDN_DV = 128
CONV_K = 4
SWA_HEADS = 8
SWA_DH = 64
DILATED_BRANCHES = ((128, 1), (512, 4), (2048, 16))
SWA_BLOCK = 128
REL_BUCKETS = 32
REL_MAX_DIST = 2048
N_EXPERTS = 32
TOP_K = 4
D_FF = D_MODEL
SWIGLU_LIMIT = 7.0
SWIGLU_ALPHA = 1.702
MOE_BLOCK = 256

LANES = 128
GDN_W = GDN_HEADS * GDN_DK
SWA_W = SWA_HEADS * SWA_DH
SWA_SLABS = SWA_W // LANES
CONV_DIM = 3 * GDN_W
COL_QKV = 0
COL_GATE = COL_QKV + CONV_DIM
COL_BA = COL_GATE + GDN_W
COL_QB = COL_BA + LANES
COL_KB = COL_QB + SWA_W
COL_VB = COL_KB + SWA_W
IN_PACKED = COL_VB + SWA_W

VMEM_LIMIT = 56 * 1024 * 1024

NEG_INF = float("-inf")


def _cparams(sem, **kw):
    return pltpu.CompilerParams(dimension_semantics=sem, vmem_limit_bytes=VMEM_LIMIT, **kw)


def _bdot(a, b):
    return jnp.dot(a.astype(BF16), b.astype(BF16), preferred_element_type=F32)


def _bdot_nt(a, b):
    return lax.dot_general(a.astype(BF16), b.astype(BF16), (((1,), (1,)), ((), ())),
                           preferred_element_type=F32)


def _bdot_tn(a, b):
    return lax.dot_general(a.astype(BF16), b.astype(BF16), (((0,), (0,)), ((), ())),
                           preferred_element_type=F32)


def _split3(x):
    hi = x.astype(BF16)
    r1 = x - hi.astype(F32)
    mid = r1.astype(BF16)
    lo = (r1 - mid.astype(F32)).astype(BF16)
    return hi, mid, lo


def _dot_split3(a_bf16, x):
    return sum(jnp.dot(a_bf16, part, preferred_element_type=F32) for part in _split3(x))


def _silu(x):
    return x * jax.nn.sigmoid(x)


def _in_proj_body(x_ref, nw_ref, w_ref, qnw_ref, knw_ref, hsum_ref,
                  qkv_ref, gate_ref, ba_ref, qb_ref, kb_ref, vb_ref):
    x = x_ref[...]
    ms = jnp.mean(x * x, axis=-1, keepdims=True)
    xn = (x * lax.rsqrt(ms + EPS) * nw_ref[...]).astype(BF16)

    def seg(lo, width):
        return jnp.dot(xn, w_ref[:, lo:lo + width], preferred_element_type=F32)

    qkv_ref[...] = seg(COL_QKV, CONV_DIM)
    gate_ref[...] = seg(COL_GATE, GDN_W)
    ba_ref[...] = seg(COL_BA, LANES)

    def head_rmsnorm(z, w_row, scale):
        sq = z * z
        hi = sq.astype(BF16)
        lo = (sq - hi.astype(F32)).astype(BF16)
        hmean = (jnp.dot(hi, hsum_ref[...], preferred_element_type=F32)
                 + jnp.dot(lo, hsum_ref[...], preferred_element_type=F32))
        return z * lax.rsqrt(hmean + EPS) * w_row * scale

    def put_slabs(ref, z):
        for j in range(SWA_SLABS):
            ref[j] = z[:, j * LANES:(j + 1) * LANES]

    put_slabs(qb_ref, head_rmsnorm(seg(COL_QB, SWA_W), qnw_ref[...], SWA_DH ** -0.5))
    put_slabs(kb_ref, head_rmsnorm(seg(COL_KB, SWA_W), knw_ref[...], 1.0))
    put_slabs(vb_ref, seg(COL_VB, SWA_W))


def _in_proj(x2, norm1_w, w_packed, q_norm_w, k_norm_w, tm=512):
    t = x2.shape[0]
    hsum = jnp.asarray(np.kron(np.eye(SWA_HEADS), np.ones((SWA_DH, SWA_DH))) / SWA_DH, BF16)
    row = lambda i: (i, 0)
    full = lambda i: (0, 0)
    slab = lambda i: (0, i, 0)
    return pl.pallas_call(
        _in_proj_body,
        grid=(t // tm,),
        in_specs=[
            pl.BlockSpec((tm, D_MODEL), row),
            pl.BlockSpec((1, D_MODEL), full),
            pl.BlockSpec((D_MODEL, IN_PACKED), full),
            pl.BlockSpec((1, SWA_W), full),
            pl.BlockSpec((1, SWA_W), full),
            pl.BlockSpec((SWA_W, SWA_W), full),
        ],
        out_specs=[
            pl.BlockSpec((tm, CONV_DIM), row),
            pl.BlockSpec((tm, GDN_W), row),
            pl.BlockSpec((tm, LANES), row),
            pl.BlockSpec((SWA_SLABS, tm, LANES), slab),
            pl.BlockSpec((SWA_SLABS, tm, LANES), slab),
            pl.BlockSpec((SWA_SLABS, tm, LANES), slab),
        ],
        out_shape=[
            jax.ShapeDtypeStruct((t, CONV_DIM), F32),
            jax.ShapeDtypeStruct((t, GDN_W), F32),
            jax.ShapeDtypeStruct((t, LANES), F32),
            jax.ShapeDtypeStruct((SWA_SLABS, t, LANES), F32),
            jax.ShapeDtypeStruct((SWA_SLABS, t, LANES), F32),
            jax.ShapeDtypeStruct((SWA_SLABS, t, LANES), F32),
        ],
        compiler_params=_cparams(("parallel",)),
        name="in_proj",
    )(x2, norm1_w.reshape(1, D_MODEL), w_packed,
      jnp.tile(q_norm_w, SWA_HEADS).reshape(1, SWA_W), jnp.tile(k_norm_w, SWA_HEADS).reshape(1, SWA_W), hsum)


def _pack_w_in(w_in):
    o = np.cumsum([0, GDN_W, GDN_W, GDN_W, GDN_W, GDN_HEADS, GDN_HEADS, SWA_W, SWA_W, SWA_W])
    ba = jnp.pad(w_in[:, o[4]:o[6]], ((0, 0), (0, LANES - 2 * GDN_HEADS)))
    return jnp.concatenate([w_in[:, o[0]:o[4]], ba, w_in[:, o[6]:o[9]]], axis=1).astype(BF16)


def _unit_lower_inverses(a_list, n, blk):
    ii = lax.broadcasted_iota(I32, (n, n), 0)
    jj = lax.broadcasted_iota(I32, (n, n), 1)
    same = lambda sh: jnp.right_shift(ii, sh) == jnp.right_shift(jj, sh)
    eye = (ii == jj).astype(F32)
    pair = same(1)
    t_list = [eye - jnp.where(pair, a, 0.0) for a in a_list]
    sh = 1
    while (1 << sh) < blk:
        join = same(sh + 1) & jnp.logical_not(same(sh))
        te = [_bdot(t, jnp.where(join, a, 0.0)) for t, a in zip(t_list, a_list)]
        tet = [_bdot(x, t) for x, t in zip(te, t_list)]
        t_list = [t - y for t, y in zip(t_list, tet)]
        sh += 1
    return t_list


def _gdn_body(chunk, n_chunks, qkv_ref, gate_ref, ba_ref, convw_ref, alane_ref, dtlane_ref, normw_ref,
              o_ref, tail_ref, state_ref):
    c = chunk
    rows = chunk * n_chunks

    @pl.when(pl.program_id(1) == 0)
    def _():
        tail_ref[...] = jnp.zeros_like(tail_ref)
        state_ref[...] = jnp.zeros_like(state_ref)

    x = qkv_ref[0]
    xs = jnp.concatenate([tail_ref[...], x], axis=0)
    w = convw_ref[...]
    conv = (xs[5:5 + rows] * w[0:1] + xs[6:6 + rows] * w[1:2] + xs[7:7 + rows] * w[2:3] + x * w[3:4])
    tail_ref[...] = x[rows - 8:rows]
    y = _silu(conv)

    ba = ba_ref[0]
    beta_all = jax.nn.sigmoid(ba)
    z = ba + dtlane_ref[...]
    softplus = jnp.maximum(z, 0.0) + jnp.log(1.0 + jnp.exp(-jnp.abs(z)))
    g_all = -alane_ref[...] * softplus
    ri = lax.broadcasted_iota(I32, (rows, rows), 0)
    ci = lax.broadcasted_iota(I32, (rows, rows), 1)
    csh = c.bit_length() - 1
    tri = ((ri >= ci) & (jnp.right_shift(ri, csh) == jnp.right_shift(ci, csh))).astype(F32)
    gc_all = _dot_split3(tri.astype(BF16), g_all)

    heads = range(GDN_HEADS)
    qn, kn, vv = [], [], []
    for h in heads:
        lo = h * GDN_DK
        q = y[:, lo:lo + GDN_DK]
        k = y[:, GDN_W + lo:GDN_W + lo + GDN_DK]
        qn.append(q * lax.rsqrt(jnp.sum(q * q, axis=-1, keepdims=True) + EPS) * (GDN_DK ** -0.5))
        kn.append(k * lax.rsqrt(jnp.sum(k * k, axis=-1, keepdims=True) + EPS))
        vv.append(y[:, 2 * GDN_W + lo:2 * GDN_W + lo + GDN_DV])

    hs = GDN_HEADS * c
    hsh = csh
    ii = lax.broadcasted_iota(I32, (hs, hs), 0)
    jj = lax.broadcasted_iota(I32, (hs, hs), 1)
    same_head = jnp.right_shift(ii, hsh) == jnp.right_shift(jj, hsh)
    row_head = jnp.right_shift(lax.broadcasted_iota(I32, (hs, GDN_DV), 0), hsh)

    def stack(parts, r0):
        return jnp.concatenate([p[r0:r0 + c] for p in parts], axis=0)

    def col(src, lane0, r0):
        return jnp.concatenate([src[r0:r0 + c, lane0 + h:lane0 + h + 1] for h in heads], axis=0)

    def diag_blocks(p):
        out = jnp.where(row_head == 0, p[:, 0:GDN_DV], 0.0)
        for h in range(1, GDN_HEADS):
            out = out + jnp.where(row_head == h, p[:, h * GDN_DV:(h + 1) * GDN_DV], 0.0)
        return out

    q_st, u_rhs, kdec, cdecay, a_kk, a_qk = ([] for _ in range(6))
    for n in range(n_chunks):
        r0 = n * c
        k = stack(kn, r0)
        q = stack(qn, r0)
        v = stack(vv, r0)
        beta = col(beta_all, 0, r0)
        gcol = col(gc_all, GDN_HEADS, r0)
        grow = jnp.transpose(jnp.broadcast_to(gcol, (hs, LANES)))[0:1, :]
        glast = jnp.concatenate(
            [jnp.broadcast_to(gc_all[r0 + c - 1:r0 + c, GDN_HEADS + h:GDN_HEADS + h + 1], (c, 1)) for h in heads],
            axis=0)
        dec = jnp.exp(jnp.where(same_head & (ii >= jj), gcol - grow, NEG_INF))
        eg = jnp.exp(gcol)
        kb = k * beta
        prod = _bdot_nt(jnp.concatenate([kb, q], axis=0), k)
        a_kk.append(jnp.where(ii > jj, prod[:hs] * dec, 0.0))
        a_qk.append(prod[hs:] * dec)
        q_st.append(q * eg)
        u_rhs.append(jnp.concatenate([v * beta, kb * eg], axis=-1))
        kdec.append(k * jnp.exp(glast - gcol))
        cdecay.append(jnp.concatenate(
            [jnp.broadcast_to(jnp.exp(gc_all[r0 + c - 1:r0 + c, GDN_HEADS + h:GDN_HEADS + h + 1]), (1, GDN_DV))
             for h in heads], axis=-1))
    t_inv = _unit_lower_inverses(a_kk, hs, c)
    sols = [_bdot(t, r) for t, r in zip(t_inv, u_rhs)]

    gate = gate_ref[0]
    normw = normw_ref[...]
    s_cat = state_ref[...]
    for n in range(n_chunks):
        r0 = n * c
        u, wk = sols[n][:, :GDN_DV], sols[n][:, GDN_DV:]
        p = _bdot(jnp.concatenate([wk, q_st[n]], axis=0), s_cat)
        v_new = u - diag_blocks(p[:hs])
        o = diag_blocks(p[hs:]) + _bdot(a_qk[n], v_new)
        v_bd = jnp.concatenate([jnp.where(row_head == h, v_new, 0.0) for h in heads], axis=-1)
        s_cat = s_cat * cdecay[n] + _bdot_tn(kdec[n], v_bd)
        o = o * lax.rsqrt(jnp.mean(o * o, axis=-1, keepdims=True) + EPS) * normw
        for h in heads:
            lo = h * GDN_DV
            o_ref[0, r0:r0 + c, lo:lo + GDN_DV] = o[h * c:(h + 1) * c] * _silu(gate[r0:r0 + c, lo:lo + GDN_DV])
    state_ref[...] = s_cat


def _gdn(qkv, gate, ba, conv_w, a_log, dt_bias, gdn_norm_w, chunk=64, n_chunks=4):
    b, s, _ = qkv.shape
    rows = chunk * n_chunks
    pad = jnp.zeros((LANES - 2 * GDN_HEADS,), F32)
    alane = jnp.concatenate([jnp.zeros((GDN_HEADS,), F32), jnp.exp(a_log.astype(F32)), pad]).reshape(1, LANES)
    dtlane = jnp.concatenate([jnp.zeros((GDN_HEADS,), F32), dt_bias.astype(F32), pad]).reshape(1, LANES)
    blk = lambda i, j: (i, j, 0)
    full = lambda i, j: (0, 0)
    return pl.pallas_call(
        functools.partial(_gdn_body, chunk, n_chunks),
        grid=(b, s // rows),
        in_specs=[
            pl.BlockSpec((1, rows, CONV_DIM), blk),
            pl.BlockSpec((1, rows, GDN_W), blk),
            pl.BlockSpec((1, rows, LANES), blk),
            pl.BlockSpec((CONV_K, CONV_DIM), full),
            pl.BlockSpec((1, LANES), full),
            pl.BlockSpec((1, LANES), full),
            pl.BlockSpec((1, GDN_DV), full),
        ],
        out_specs=pl.BlockSpec((1, rows, GDN_W), blk),
        out_shape=jax.ShapeDtypeStruct((b, s, GDN_W), F32),
        scratch_shapes=[pltpu.VMEM((8, CONV_DIM), F32), pltpu.VMEM((GDN_DK, GDN_HEADS * GDN_DV), F32)],
        compiler_params=_cparams(("parallel", "arbitrary")),
        name="gdn",
    )(qkv, gate, ba, conv_w.astype(F32), alane, dtlane, gdn_norm_w.astype(F32).reshape(1, GDN_DV))


def _t5_bucket(dist):
    max_exact = REL_BUCKETS // 2
    n = np.maximum(dist, 0)
    large = max_exact + (np.log(np.maximum(n, 1) / max_exact) / math.log(REL_MAX_DIST / max_exact)
                         * (REL_BUCKETS - max_exact)).astype(np.int32)
    large = np.minimum(large, REL_BUCKETS - 1)
    return np.where(n < max_exact, n, large).astype(np.int32)


def _branch_bias(rel_bias, window, dilation):
    kj = np.arange(2 * SWA_BLOCK)[:, None]
    qi = np.arange(SWA_BLOCK)[None, :]
    steps = qi + SWA_BLOCK - kj
    valid = (steps >= 0) & (steps <= window // dilation)
    onehot = np.eye(REL_BUCKETS, dtype=np.float32)[_t5_bucket(steps * dilation).reshape(-1)]
    bias = jnp.dot(jnp.asarray(onehot), rel_bias.astype(F32), precision=lax.Precision.HIGHEST)
    bias = bias.reshape(2 * SWA_BLOCK, SWA_BLOCK, SWA_HEADS).transpose(2, 0, 1)
    bias = jnp.where(jnp.asarray(valid)[None], bias, NEG_INF)
    return jnp.concatenate([bias[:, :SWA_BLOCK], jnp.full((SWA_HEADS, SWA_BLOCK, SWA_BLOCK), NEG_INF, F32),
                            bias[:, SWA_BLOCK:]], axis=1)


SWA_SUPER = 2048
STAT_ROWS = 16


def _swa_body(q_ref, k_ref, v_ref, bias_ref, spread_ref, o_ref, acco, accl):
    sb = pl.program_id(1)
    n_qblk = SWA_SUPER // SWA_BLOCK
    stat_pad = jnp.zeros((LANES - SWA_HEADS, SWA_BLOCK), F32)
    last_branch = len(DILATED_BRANCHES) - 1

    for bi, (_, d) in enumerate(DILATED_BRANCHES):
        dsh = d.bit_length() - 1
        per_class = n_qblk // d

        def rows(start, d=d):
            if d > 1:
                return pl.ds(start, SWA_BLOCK, stride=d)
            return pl.ds(pl.multiple_of(start, SWA_BLOCK), SWA_BLOCK)

        def qblock(idx, carry, bi=bi, d=d, dsh=dsh, per_class=per_class, rows=rows):
            r = jnp.bitwise_and(idx, d - 1)
            n = jnp.right_shift(idx, dsh)
            loc = r + n * (d * SWA_BLOCK)
            cur = sb * SWA_SUPER + loc
            first = (sb * per_class + n) == 0
            prev = jnp.maximum(cur - d * SWA_BLOCK, r)
            poff = pl.multiple_of(jnp.where(first, SWA_BLOCK, 0), SWA_BLOCK)
            scores, values = [], []
            for j in range(SWA_SLABS):
                qb = q_ref[j, rows(loc), :].astype(BF16)
                kw = jnp.concatenate([k_ref[j, rows(prev), :], k_ref[j, rows(cur), :]], axis=0).astype(BF16)
                vw = jnp.concatenate([v_ref[j, rows(prev), :], v_ref[j, rows(cur), :]], axis=0).astype(BF16)
                for lo in range(0, LANES, SWA_DH):
                    scores.append(_bdot_nt(kw[:, lo:lo + SWA_DH], qb[:, lo:lo + SWA_DH]))
                    values.append(vw[:, lo:lo + SWA_DH])
            mxs, dens, probs = [], [], []
            for h in range(SWA_HEADS):
                bias = jnp.concatenate([bias_ref[bi, h, pl.ds(poff, SWA_BLOCK), :],
                                        bias_ref[bi, h, 2 * SWA_BLOCK:3 * SWA_BLOCK, :]], axis=0)
                st = scores[h] + bias
                mx = jnp.max(st, axis=0, keepdims=True)
                p = jnp.exp(st - mx)
                mxs.append(mx)
                dens.append(jnp.sum(p, axis=0, keepdims=True))
                probs.append(p.astype(BF16))
            outs = [_bdot_tn(values[h], probs[h]) for h in range(SWA_HEADS)]
            mx = jnp.concatenate(mxs, axis=0)
            den = jnp.concatenate(dens, axis=0)
            lse = mx + jnp.log(den)
            if bi == 0:
                scale = 1.0 / den
                new = lse
            else:
                lacc = jnp.transpose(accl[rows(loc), :])[0:SWA_HEADS]
                m2 = jnp.maximum(lacc, lse)
                ea = jnp.exp(lacc - m2)
                tot = ea + jnp.exp(lse - m2)
                scale = jnp.exp(mx - m2) / tot
                keep = ea / tot
                new = m2 + jnp.log(tot)
            o_tok = jnp.transpose(
                jnp.concatenate([outs[h] * scale[h:h + 1] for h in range(SWA_HEADS)], axis=0))
            if bi > 0:
                kpad = jnp.concatenate([keep, jnp.zeros((STAT_ROWS - SWA_HEADS, SWA_BLOCK), F32)], axis=0)
                hi = kpad.astype(BF16)
                lo2 = (kpad - hi.astype(F32)).astype(BF16)
                spread = spread_ref[...]
                keep_tok = _bdot_tn(hi, spread) + _bdot_tn(lo2, spread)
            for j in range(SWA_SLABS):
                o_slab = o_tok[:, j * LANES:(j + 1) * LANES]
                if bi > 0:
                    o_slab = acco[j, rows(loc), :] * keep_tok[:, j * LANES:(j + 1) * LANES] + o_slab
                acco[j, rows(loc), :] = o_slab
            if bi < last_branch:
                accl[rows(loc), :] = jnp.transpose(jnp.concatenate([new, stat_pad], axis=0))
            return carry

        lax.fori_loop(0, n_qblk, qblock, 0)
    for j in range(SWA_SLABS):
        o_ref[:, j * LANES:(j + 1) * LANES] = acco[j].astype(BF16)


def _dilated_attention(qb, kb, vb, rel_bias, b, s):
    bias = jnp.stack([_branch_bias(rel_bias, w, d) for w, d in DILATED_BRANCHES])
    spread = np.zeros((STAT_ROWS, SWA_W), np.float32)
    for h in range(SWA_HEADS):
        spread[h, h * SWA_DH:(h + 1) * SWA_DH] = 1.0
    once = pl.Buffered(1)
    spans = s // SWA_SUPER
    span = lambda i, j: (0, i * spans + j, 0)
    seq = lambda i, j: (0, i, 0)
    return pl.pallas_call(
        _swa_body,
        grid=(b, spans),
        in_specs=[
            pl.BlockSpec((SWA_SLABS, SWA_SUPER, LANES), span),
            pl.BlockSpec((SWA_SLABS, s, LANES), seq, pipeline_mode=once),
            pl.BlockSpec((SWA_SLABS, s, LANES), seq, pipeline_mode=once),
            pl.BlockSpec(bias.shape, lambda i, j: (0, 0, 0, 0), pipeline_mode=once),
            pl.BlockSpec((STAT_ROWS, SWA_W), lambda i, j: (0, 0)),
        ],
        out_specs=pl.BlockSpec((SWA_SUPER, SWA_W), lambda i, j: (i * spans + j, 0)),
        out_shape=jax.ShapeDtypeStruct((b * s, SWA_W), BF16),
        scratch_shapes=[pltpu.VMEM((SWA_SLABS, SWA_SUPER, LANES), F32), pltpu.VMEM((SWA_SUPER, LANES), F32)],
        compiler_params=_cparams(("parallel", "arbitrary")),
        name="swa",
    )(qb, kb, vb, bias, jnp.asarray(spread, BF16))


TOKEN_TILE = D_MODEL // LANES


def _put_token_tiles(ref, rows):
    m = rows.shape[0]
    for s in range(TOKEN_TILE):
        ref[pl.ds(s, m, stride=TOKEN_TILE), :] = rows[:, s * LANES:(s + 1) * LANES]


def _get_token_tiles(ref, m, base=0):
    return jnp.concatenate([ref[pl.ds(base + s, m, stride=TOKEN_TILE), :] for s in range(TOKEN_TILE)], axis=-1)


def _out_proj_body(x_ref, oa_ref, ob_ref, w_ref, nw_ref, wr_ref, br_ref, h_ref, hn_ref, e_ref, g_ref):
    h = (x_ref[...] + _bdot(oa_ref[...], w_ref[0:GDN_W, :]) + _bdot(ob_ref[...], w_ref[GDN_W:, :]))
    h_ref[...] = h
    hn = h * lax.rsqrt(jnp.mean(h * h, axis=-1, keepdims=True) + EPS) * nw_ref[...]
    _put_token_tiles(hn_ref, hn)
    hn_hi = hn.astype(BF16)
    hn_lo = (hn - hn_hi.astype(F32)).astype(BF16)
    logits = (jnp.dot(hn_hi, wr_ref[0], preferred_element_type=F32)
              + jnp.dot(hn_hi, wr_ref[1], preferred_element_type=F32)
              + jnp.dot(hn_lo, wr_ref[0], preferred_element_type=F32) + br_ref[...])
    lane = lax.broadcasted_iota(I32, logits.shape, 1)
    e_tile = jnp.zeros(logits.shape, I32)
    g_tile = jnp.zeros(logits.shape, F32)
    top0 = None
    den = None
    for k in range(TOP_K):
        mx = jnp.max(logits, axis=-1, keepdims=True)
        idx = jnp.min(jnp.where(logits == mx, lane, LANES), axis=-1, keepdims=True)
        logits = jnp.where(lane == idx, NEG_INF, logits)
        if k == 0:
            top0 = mx
        ex = jnp.exp(mx - top0)
        den = ex if k == 0 else den + ex
        e_tile = jnp.where(lane == k, idx, e_tile)
        g_tile = jnp.where(lane == k, ex, g_tile)
    e_ref[...] = e_tile
    g_ref[...] = g_tile / den


def _out_proj(x2, oa, ob, w_out, norm2_w, w_router, b_router, tm=512):
    t = x2.shape[0]
    wr = jnp.pad(w_router.astype(F32), ((0, 0), (0, LANES - N_EXPERTS)))
    wr_hi = wr.astype(BF16)
    wr = jnp.stack([wr_hi, (wr - wr_hi.astype(F32)).astype(BF16)])
    br = jnp.concatenate([b_router.astype(F32), jnp.full((LANES - N_EXPERTS,), NEG_INF, F32)]).reshape(1, LANES)
    row = lambda i: (i, 0)
    full = lambda i: (0, 0)
    return pl.pallas_call(
        _out_proj_body,
        grid=(t // tm,),
        in_specs=[
            pl.BlockSpec((tm, D_MODEL), row),
            pl.BlockSpec((tm, GDN_W), row),
            pl.BlockSpec((tm, SWA_W), row),
            pl.BlockSpec((GDN_W + SWA_W, D_MODEL), full),
            pl.BlockSpec((1, D_MODEL), full),
            pl.BlockSpec((2, D_MODEL, LANES), lambda i: (0, 0, 0)),
            pl.BlockSpec((1, LANES), full),
        ],
        out_specs=[
            pl.BlockSpec((tm, D_MODEL), row),
            pl.BlockSpec((tm * TOKEN_TILE, LANES), row),
            pl.BlockSpec((tm, LANES), row),
            pl.BlockSpec((tm, LANES), row),
        ],
        out_shape=[
            jax.ShapeDtypeStruct((t, D_MODEL), F32),
            jax.ShapeDtypeStruct((t * TOKEN_TILE, LANES), F32),
            jax.ShapeDtypeStruct((t, LANES), I32),
            jax.ShapeDtypeStruct((t, LANES), F32),
        ],
        compiler_params=_cparams(("parallel",)),
        name="out_proj",
    )(x2, oa, ob, w_out.astype(BF16), norm2_w.astype(F32).reshape(1, D_MODEL), wr, br)


def _moe_plan(top_e):
    t = top_e.shape[0]
    n = t * TOP_K
    p_rows = n + N_EXPERTS * MOE_BLOCK
    nblk = p_rows // MOE_BLOCK
    e_flat = top_e.reshape(-1)
    experts = jnp.arange(N_EXPERTS, dtype=I32)
    counts = jnp.sum((e_flat[:, None] == experts[None, :]).astype(I32), axis=0)
    padded = (counts + MOE_BLOCK - 1) // MOE_BLOCK * MOE_BLOCK
    pad_end = jnp.cumsum(padded)
    cand = jnp.arange(MOE_BLOCK, dtype=I32)[None, :] < (padded - counts)[:, None]
    cand_key = jnp.where(cand, 2 * experts[:, None] + 1, 2 * N_EXPERTS).reshape(-1)
    keys = jnp.concatenate([2 * e_flat, cand_key])
    vals = jnp.concatenate([jnp.arange(n, dtype=I32), jnp.full((N_EXPERTS * MOE_BLOCK,), -1, I32)])
    _, src = lax.sort_key_val(keys, vals)
    is_pad = src < 0
    pad_rank = jnp.cumsum(is_pad.astype(I32)) - 1
    tok = jnp.right_shift(src, TOP_K.bit_length() - 1)
    gather_tok = jnp.where(is_pad, 0, tok)
    scatter_row = jnp.where(is_pad, n + pad_rank, jnp.bitwise_and(src, TOP_K - 1) * t + tok)
    blk_start = jnp.arange(nblk, dtype=I32) * MOE_BLOCK
    blk_expert = jnp.minimum(jnp.sum((blk_start[:, None] >= pad_end[None, :]).astype(I32), axis=1), N_EXPERTS - 1)
    return gather_tok.reshape(nblk, 1, MOE_BLOCK), scatter_row.reshape(nblk, 1, MOE_BLOCK), blk_expert


MOE_COL_CHUNK = 256


def _moe_body(bexp_ref, gidx_ref, sidx_ref, hn_ref, wgu_ref, bgu_ref, wdn_ref, bdn_ref, y_ref,
              xbuf, ybuf, wgu_bf, wdn_bf, gsem, ssem):
    i = pl.program_id(0)
    last = pl.num_programs(0) - 1
    slot = i % 2

    def gather_copy(tok, s, r):
        return pltpu.make_async_copy(hn_ref.at[tok], xbuf.at[s, pl.ds(r * TOKEN_TILE, TOKEN_TILE)], gsem.at[s])

    def scatter_copy(row, r):
        return pltpu.make_async_copy(ybuf.at[pl.ds(r * TOKEN_TILE, TOKEN_TILE)], y_ref.at[row], ssem)

    @pl.when(i == 0)
    def _():
        for r in range(MOE_BLOCK):
            gather_copy(gidx_ref[0, 0, r], 0, r).start()
        ybuf[...] = jnp.zeros_like(ybuf)

    @pl.when(i >= 0)
    def _():
        for r in range(MOE_BLOCK):
            scatter_copy(sidx_ref[0, 0, r], r).start(priority=r % 2)
            gather_copy(gidx_ref[1, 0, r], 1 - slot, r).start(priority=r % 2)

    @pl.when(jnp.logical_or(i == 0, bexp_ref[i] != bexp_ref[jnp.maximum(i - 1, 0)]))
    def _():
        wgu_bf[...] = wgu_ref[0].astype(BF16)
        wdn_bf[...] = wdn_ref[0].astype(BF16)

    for r in range(MOE_BLOCK):
        gather_copy(0, slot, r).wait()
    x = _get_token_tiles(xbuf.at[slot], MOE_BLOCK).astype(BF16)
    n_col = D_FF // MOE_COL_CHUNK
    acts = []
    for c in range(n_col):
        lo = c * MOE_COL_CHUNK
        g = jnp.dot(x, wgu_bf[:, lo:lo + MOE_COL_CHUNK], preferred_element_type=F32)
        u = jnp.dot(x, wgu_bf[:, D_FF + lo:D_FF + lo + MOE_COL_CHUNK], preferred_element_type=F32)
        g = jnp.minimum(g + bgu_ref[0, :, lo:lo + MOE_COL_CHUNK], SWIGLU_LIMIT)
        u = jnp.clip(u + bgu_ref[0, :, D_FF + lo:D_FF + lo + MOE_COL_CHUNK], -SWIGLU_LIMIT, SWIGLU_LIMIT)
        acts.append(((u + 1.0) * g * jax.nn.sigmoid(SWIGLU_ALPHA * g)).astype(BF16))
    act = jnp.concatenate(acts, axis=-1)
    outs = []
    for c in range(D_MODEL // MOE_COL_CHUNK):
        lo = c * MOE_COL_CHUNK
        outs.append(jnp.dot(act, wdn_bf[:, lo:lo + MOE_COL_CHUNK], preferred_element_type=F32)
                    + bdn_ref[0, :, lo:lo + MOE_COL_CHUNK])
    y = jnp.concatenate(outs, axis=-1)

    for r in range(MOE_BLOCK):
        scatter_copy(0, r).wait()
    _put_token_tiles(ybuf, y)

    @pl.when(i == last)
    def _():
        for r in range(MOE_BLOCK):
            scatter_copy(sidx_ref[1, 0, r], r).start()
        for r in range(MOE_BLOCK):
            scatter_copy(0, r).wait()
        for r in range(MOE_BLOCK):
            gather_copy(0, 1 - slot, r).wait()


def _moe_ffn(hn, plan, w_gu, b_gu, w_dn, b_dn):
    t = hn.shape[0] // TOKEN_TILE
    gather_tok, scatter_row, blk_expert = plan
    nblk = gather_tok.shape[0]
    n_rows = t * TOP_K + (N_EXPERTS + 1) * MOE_BLOCK
    gnext = jnp.concatenate([gather_tok[1:], gather_tok[-1:]], axis=0)
    gpair = jnp.concatenate([gather_tok, gnext], axis=1).reshape(nblk * 2, 1, MOE_BLOCK)
    before_first = (n_rows - MOE_BLOCK + jnp.arange(MOE_BLOCK, dtype=I32)).reshape(1, 1, MOE_BLOCK)
    sprev = jnp.concatenate([before_first, scatter_row[:-1]], axis=0)
    spair = jnp.concatenate([sprev, scatter_row], axis=1).reshape(nblk * 2, 1, MOE_BLOCK)
    wsel = lambda i, be: (be[i], 0, 0)
    pair = lambda i, be: (i, 0, 0)
    grid_spec = pltpu.PrefetchScalarGridSpec(
        num_scalar_prefetch=1,
        grid=(nblk,),
        in_specs=[
            pl.BlockSpec((2, 1, MOE_BLOCK), pair, memory_space=pltpu.SMEM),
            pl.BlockSpec((2, 1, MOE_BLOCK), pair, memory_space=pltpu.SMEM),
            pl.BlockSpec(memory_space=pl.ANY),
            pl.BlockSpec((1, D_MODEL, 2 * D_FF), wsel),
            pl.BlockSpec((1, 1, 2 * D_FF), wsel),
            pl.BlockSpec((1, D_FF, D_MODEL), wsel),
            pl.BlockSpec((1, 1, D_MODEL), wsel),
        ],
        out_specs=pl.BlockSpec(memory_space=pl.ANY),
        scratch_shapes=[
            pltpu.VMEM((2, MOE_BLOCK * TOKEN_TILE, LANES), F32),
            pltpu.VMEM((MOE_BLOCK * TOKEN_TILE, LANES), F32),
            pltpu.VMEM((D_MODEL, 2 * D_FF), BF16),
            pltpu.VMEM((D_FF, D_MODEL), BF16),
            pltpu.SemaphoreType.DMA((2,)),
            pltpu.SemaphoreType.DMA(()),
        ],
    )
    return pl.pallas_call(
        _moe_body,
        grid_spec=grid_spec,
        out_shape=jax.ShapeDtypeStruct((n_rows, TOKEN_TILE, LANES), F32),
        compiler_params=_cparams(("arbitrary",)),
        name="moe_ffn",
    )(blk_expert, gpair, spair, hn.reshape(t, TOKEN_TILE, LANES),
      w_gu.astype(F32), b_gu.astype(F32).reshape(N_EXPERTS, 1, 2 * D_FF),
      w_dn.astype(F32), b_dn.astype(F32).reshape(N_EXPERTS, 1, D_MODEL))


def _combine_body(h_ref, g_ref, *refs):
    y_refs, o_ref = refs[:TOP_K], refs[TOP_K]
    g = g_ref[...]
    m = g.shape[0]
    moe = g[:, 0:1] * _get_token_tiles(y_refs[0], m)
    for k in range(1, TOP_K):
        moe = moe + g[:, k:k + 1] * _get_token_tiles(y_refs[k], m)
    o_ref[...] = h_ref[...] + moe


def _combine(h, y_slots, gates, tm=512):
    t = h.shape[0]
    y2 = y_slots.reshape(-1, LANES)
    row = lambda i: (i, 0)
    kth = lambda k: (lambda i: (k * (t // tm) + i, 0))
    return pl.pallas_call(
        _combine_body,
        grid=(t // tm,),
        in_specs=[pl.BlockSpec((tm, D_MODEL), row), pl.BlockSpec((tm, LANES), row)]
        + [pl.BlockSpec((tm * TOKEN_TILE, LANES), kth(k)) for k in range(TOP_K)],
        out_specs=pl.BlockSpec((tm, D_MODEL), row),
        out_shape=jax.ShapeDtypeStruct((t, D_MODEL), F32),
        compiler_params=_cparams(("parallel",)),
        name="combine",
    )(h, gates, *([y2] * TOP_K))


def _layer(x, norm1_w, w_in, conv_w, a_log, dt_bias, gdn_norm_w, q_norm_w, k_norm_w, rel_bias,
           w_out, norm2_w, w_router, b_router, w_gate_up, b_gate_up, w_down, b_down):
    b, s, d = x.shape
    t = b * s
    x2 = x.reshape(t, d)
    qkv, gate, ba, qb, kb, vb = _in_proj(x2, norm1_w.astype(F32), _pack_w_in(w_in),
                                         q_norm_w.astype(F32), k_norm_w.astype(F32))
    oa = _gdn(qkv.reshape(b, s, CONV_DIM), gate.reshape(b, s, GDN_W), ba.reshape(b, s, LANES),
              conv_w, a_log, dt_bias, gdn_norm_w)
    ob = _dilated_attention(qb, kb, vb, rel_bias, b, s)
    h, hn, top_e, gates = _out_proj(x2, oa.reshape(t, GDN_W), ob, w_out, norm2_w, w_router, b_router)
    plan = _moe_plan(top_e[:, :TOP_K])
    y_slots = _moe_ffn(hn, plan, w_gate_up, b_gate_up, w_down, b_down)
    return _combine(h, y_slots, gates).reshape(b, s, d)


def kernel(x, norm1_w, w_in, conv_w, a_log, dt_bias, gdn_norm_w, q_norm_w, k_norm_w, rel_bias, w_out, norm2_w,
           w_router, b_router, w_gate_up, b_gate_up, w_down, b_down):
    h = x
    for l in range(norm1_w.shape[0]):
        h = _layer(h, norm1_w[l], w_in[l], conv_w[l], a_log[l], dt_bias[l], gdn_norm_w[l], q_norm_w[l],
                   k_norm_w[l], rel_bias, w_out[l], norm2_w[l], w_router[l], b_router[l], w_gate_up[l],
                   b_gate_up[l], w_down[l], b_down[l])
    return h
```

```python
import functools
import math

import numpy as np
import jax
import jax.numpy as jnp
from jax import lax
from jax.experimental import pallas as pl
from jax.experimental.pallas import tpu as pltpu

F32 = jnp.float32
BF16 = jnp.bfloat16
I32 = jnp.int32

EPS = 1e-6
D_MODEL = 1024
GDN_HEADS = 4
GDN_DK = 128
GDN_DV = 128
CONV_K = 4
SWA_HEADS = 8
SWA_DH = 64
DILATED_BRANCHES = ((128, 1), (512, 4), (2048, 16))
SWA_BLOCK = 128
REL_BUCKETS = 32
REL_MAX_DIST = 2048
N_EXPERTS = 32
TOP_K = 4
D_FF = D_MODEL
SWIGLU_LIMIT = 7.0
SWIGLU_ALPHA = 1.702
MOE_BLOCK = 512

LANES = 128
GDN_W = GDN_HEADS * GDN_DK
SWA_W = SWA_HEADS * SWA_DH
SWA_SLABS = SWA_W // LANES
CONV_DIM = 3 * GDN_W
COL_QKV = 0
COL_GATE = COL_QKV + CONV_DIM
COL_BA = COL_GATE + GDN_W
COL_QB = COL_BA + LANES
COL_KB = COL_QB + SWA_W
COL_VB = COL_KB + SWA_W
IN_PACKED = COL_VB + SWA_W

VMEM_LIMIT = 56 * 1024 * 1024

NEG_INF = float("-inf")


def _cparams(sem, **kw):
    return pltpu.CompilerParams(dimension_semantics=sem, vmem_limit_bytes=VMEM_LIMIT, **kw)


def _bdot(a, b):
    return jnp.dot(a.astype(BF16), b.astype(BF16), preferred_element_type=F32)


def _bdot_nt(a, b):
    return lax.dot_general(a.astype(BF16), b.astype(BF16), (((1,), (1,)), ((), ())),
                           preferred_element_type=F32)


def _bdot_tn(a, b):
    return lax.dot_general(a.astype(BF16), b.astype(BF16), (((0,), (0,)), ((), ())),
                           preferred_element_type=F32)


def _split3(x):
    hi = x.astype(BF16)
    r1 = x - hi.astype(F32)
    mid = r1.astype(BF16)
    lo = (r1 - mid.astype(F32)).astype(BF16)
    return hi, mid, lo


def _dot_split3(a_bf16, x):
    return sum(jnp.dot(a_bf16, part, preferred_element_type=F32) for part in _split3(x))


def _silu(x):
    return x * jax.nn.sigmoid(x)


def _in_proj_body(x_ref, nw_ref, w_ref, qnw_ref, knw_ref, hsum_ref,
                  qkv_ref, gate_ref, ba_ref, qb_ref, kb_ref, vb_ref):
    x = x_ref[...]
    ms = jnp.mean(x * x, axis=-1, keepdims=True)
    xn = (x * lax.rsqrt(ms + EPS) * nw_ref[...]).astype(BF16)

    def seg(lo, width):
        return jnp.dot(xn, w_ref[:, lo:lo + width], preferred_element_type=F32)

    qkv_ref[...] = seg(COL_QKV, CONV_DIM)
    gate_ref[...] = seg(COL_GATE, GDN_W)
    ba_ref[...] = seg(COL_BA, LANES)

    def head_rmsnorm(z, w_row, scale):
        sq = z * z
        hi = sq.astype(BF16)
        lo = (sq - hi.astype(F32)).astype(BF16)
        hmean = (jnp.dot(hi, hsum_ref[...], preferred_element_type=F32)
                 + jnp.dot(lo, hsum_ref[...], preferred_element_type=F32))
        return z * lax.rsqrt(hmean + EPS) * w_row * scale

    def put_slabs(ref, z):
        for j in range(SWA_SLABS):
            ref[j] = z[:, j * LANES:(j + 1) * LANES]

    put_slabs(qb_ref, head_rmsnorm(seg(COL_QB, SWA_W), qnw_ref[...], SWA_DH ** -0.5))
    put_slabs(kb_ref, head_rmsnorm(seg(COL_KB, SWA_W), knw_ref[...], 1.0))
    put_slabs(vb_ref, seg(COL_VB, SWA_W))


def _in_proj(x2, norm1_w, w_packed, q_norm_w, k_norm_w, tm=512):
    t = x2.shape[0]
    hsum = jnp.asarray(np.kron(np.eye(SWA_HEADS), np.ones((SWA_DH, SWA_DH))) / SWA_DH, BF16)
    row = lambda i: (i, 0)
    full = lambda i: (0, 0)
    slab = lambda i: (0, i, 0)
    return pl.pallas_call(
        _in_proj_body,
        grid=(t // tm,),
        in_specs=[
            pl.BlockSpec((tm, D_MODEL), row),
            pl.BlockSpec((1, D_MODEL), full),
            pl.BlockSpec((D_MODEL, IN_PACKED), full),
            pl.BlockSpec((1, SWA_W), full),
            pl.BlockSpec((1, SWA_W), full),
            pl.BlockSpec((SWA_W, SWA_W), full),
        ],
        out_specs=[
            pl.BlockSpec((tm, CONV_DIM), row),
            pl.BlockSpec((tm, GDN_W), row),
            pl.BlockSpec((tm, LANES), row),
            pl.BlockSpec((SWA_SLABS, tm, LANES), slab),
            pl.BlockSpec((SWA_SLABS, tm, LANES), slab),
            pl.BlockSpec((SWA_SLABS, tm, LANES), slab),
        ],
        out_shape=[
            jax.ShapeDtypeStruct((t, CONV_DIM), F32),
            jax.ShapeDtypeStruct((t, GDN_W), F32),
            jax.ShapeDtypeStruct((t, LANES), F32),
            jax.ShapeDtypeStruct((SWA_SLABS, t, LANES), F32),
            jax.ShapeDtypeStruct((SWA_SLABS, t, LANES), F32),
            jax.ShapeDtypeStruct((SWA_SLABS, t, LANES), F32),
        ],
        compiler_params=_cparams(("parallel",)),
        name="in_proj",
    )(x2, norm1_w.reshape(1, D_MODEL), w_packed,
      jnp.tile(q_norm_w, SWA_HEADS).reshape(1, SWA_W), jnp.tile(k_norm_w, SWA_HEADS).reshape(1, SWA_W), hsum)


def _pack_w_in(w_in):
    o = np.cumsum([0, GDN_W, GDN_W, GDN_W, GDN_W, GDN_HEADS, GDN_HEADS, SWA_W, SWA_W, SWA_W])
    ba = jnp.pad(w_in[:, o[4]:o[6]], ((0, 0), (0, LANES - 2 * GDN_HEADS)))
    return jnp.concatenate([w_in[:, o[0]:o[4]], ba, w_in[:, o[6]:o[9]]], axis=1).astype(BF16)


def _unit_lower_inverses(a_list, n, blk):
    ii = lax.broadcasted_iota(I32, (n, n), 0)
    jj = lax.broadcasted_iota(I32, (n, n), 1)
    same = lambda sh: jnp.right_shift(ii, sh) == jnp.right_shift(jj, sh)
    eye = (ii == jj).astype(F32)
    pair = same(1)
    t_list = [eye - jnp.where(pair, a, 0.0) for a in a_list]
    sh = 1
    while (1 << sh) < blk:
        join = same(sh + 1) & jnp.logical_not(same(sh))
        te = [_bdot(t, jnp.where(join, a, 0.0)) for t, a in zip(t_list, a_list)]
        tet = [_bdot(x, t) for x, t in zip(te, t_list)]
        t_list = [t - y for t, y in zip(t_list, tet)]
        sh += 1
    return t_list


def _gdn_body(chunk, n_chunks, qkv_ref, gate_ref, ba_ref, convw_ref, alane_ref, dtlane_ref, normw_ref,
              o_ref, tail_ref, state_ref):
    c = chunk
    rows = chunk * n_chunks

    @pl.when(pl.program_id(1) == 0)
    def _():
        tail_ref[...] = jnp.zeros_like(tail_ref)
        state_ref[...] = jnp.zeros_like(state_ref)

    x = qkv_ref[0]
    xs = jnp.concatenate([tail_ref[...], x], axis=0)
    w = convw_ref[...]
    conv = (xs[5:5 + rows] * w[0:1] + xs[6:6 + rows] * w[1:2] + xs[7:7 + rows] * w[2:3] + x * w[3:4])
    tail_ref[...] = x[rows - 8:rows]
    y = _silu(conv)

    ba = ba_ref[0]
    beta_all = jax.nn.sigmoid(ba)
    z = ba + dtlane_ref[...]
    softplus = jnp.maximum(z, 0.0) + jnp.log(1.0 + jnp.exp(-jnp.abs(z)))
    g_all = -alane_ref[...] * softplus
    ri = lax.broadcasted_iota(I32, (rows, rows), 0)
    ci = lax.broadcasted_iota(I32, (rows, rows), 1)
    csh = c.bit_length() - 1
    tri = ((ri >= ci) & (jnp.right_shift(ri, csh) == jnp.right_shift(ci, csh))).astype(F32)
    gc_all = _dot_split3(tri.astype(BF16), g_all)

    heads = range(GDN_HEADS)
    qn, kn, vv = [], [], []
    for h in heads:
        lo = h * GDN_DK
        q = y[:, lo:lo + GDN_DK]
        k = y[:, GDN_W + lo:GDN_W + lo + GDN_DK]
        qn.append(q * lax.rsqrt(jnp.sum(q * q, axis=-1, keepdims=True) + EPS) * (GDN_DK ** -0.5))
        kn.append(k * lax.rsqrt(jnp.sum(k * k, axis=-1, keepdims=True) + EPS))
        vv.append(y[:, 2 * GDN_W + lo:2 * GDN_W + lo + GDN_DV])

    hs = GDN_HEADS * c
    hsh = csh
    ii = lax.broadcasted_iota(I32, (hs, hs), 0)
    jj = lax.broadcasted_iota(I32, (hs, hs), 1)
    same_head = jnp.right_shift(ii, hsh) == jnp.right_shift(jj, hsh)
    row_head = jnp.right_shift(lax.broadcasted_iota(I32, (hs, GDN_DV), 0), hsh)

    def stack(parts, r0):
        return jnp.concatenate([p[r0:r0 + c] for p in parts], axis=0)

    def col(src, lane0, r0):
        return jnp.concatenate([src[r0:r0 + c, lane0 + h:lane0 + h + 1] for h in heads], axis=0)

    def diag_blocks(p):
        out = jnp.where(row_head == 0, p[:, 0:GDN_DV], 0.0)
        for h in range(1, GDN_HEADS):
            out = out + jnp.where(row_head == h, p[:, h * GDN_DV:(h + 1) * GDN_DV], 0.0)
        return out

    q_st, u_rhs, kdec, cdecay, a_kk, a_qk = ([] for _ in range(6))
    for n in range(n_chunks):
        r0 = n * c
        k = stack(kn, r0)
        q = stack(qn, r0)
        v = stack(vv, r0)
        beta = col(beta_all, 0, r0)
        gcol = col(gc_all, GDN_HEADS, r0)
        grow = jnp.transpose(jnp.broadcast_to(gcol, (hs, LANES)))[0:1, :]
        glast = jnp.concatenate(
            [jnp.broadcast_to(gc_all[r0 + c - 1:r0 + c, GDN_HEADS + h:GDN_HEADS + h + 1], (c, 1)) for h in heads],
            axis=0)
        dec = jnp.exp(jnp.where(same_head & (ii >= jj), gcol - grow, NEG_INF))
        eg = jnp.exp(gcol)
        kb = k * beta
        prod = _bdot_nt(jnp.concatenate([kb, q], axis=0), k)
        a_kk.append(jnp.where(ii > jj, prod[:hs] * dec, 0.0))
        a_qk.append(prod[hs:] * dec)
        q_st.append(q * eg)
        u_rhs.append(jnp.concatenate([v * beta, kb * eg], axis=-1))
        kdec.append(k * jnp.exp(glast - gcol))
        cdecay.append(jnp.concatenate(
            [jnp.broadcast_to(jnp.exp(gc_all[r0 + c - 1:r0 + c, GDN_HEADS + h:GDN_HEADS + h + 1]), (1, GDN_DV))
             for h in heads], axis=-1))
    t_inv = _unit_lower_inverses(a_kk, hs, c)
    sols = [_bdot(t, r) for t, r in zip(t_inv, u_rhs)]

    gate = gate_ref[0]
    normw = normw_ref[...]
    s_cat = state_ref[...]
    for n in range(n_chunks):
        r0 = n * c
        u, wk = sols[n][:, :GDN_DV], sols[n][:, GDN_DV:]
        p = _bdot(jnp.concatenate([wk, q_st[n]], axis=0), s_cat)
        v_new = u - diag_blocks(p[:hs])
        o = diag_blocks(p[hs:]) + _bdot(a_qk[n], v_new)
        v_bd = jnp.concatenate([jnp.where(row_head == h, v_new, 0.0) for h in heads], axis=-1)
        s_cat = s_cat * cdecay[n] + _bdot_tn(kdec[n], v_bd)
        o = o * lax.rsqrt(jnp.mean(o * o, axis=-1, keepdims=True) + EPS) * normw
        for h in heads:
            lo = h * GDN_DV
            o_ref[0, r0:r0 + c, lo:lo + GDN_DV] = o[h * c:(h + 1) * c] * _silu(gate[r0:r0 + c, lo:lo + GDN_DV])
    state_ref[...] = s_cat


def _gdn(qkv, gate, ba, conv_w, a_log, dt_bias, gdn_norm_w, chunk=64, n_chunks=4):
    b, s, _ = qkv.shape
    rows = chunk * n_chunks
    pad = jnp.zeros((LANES - 2 * GDN_HEADS,), F32)
    alane = jnp.concatenate([jnp.zeros((GDN_HEADS,), F32), jnp.exp(a_log.astype(F32)), pad]).reshape(1, LANES)
    dtlane = jnp.concatenate([jnp.zeros((GDN_HEADS,), F32), dt_bias.astype(F32), pad]).reshape(1, LANES)
    blk = lambda i, j: (i, j, 0)
    full = lambda i, j: (0, 0)
    return pl.pallas_call(
        functools.partial(_gdn_body, chunk, n_chunks),
        grid=(b, s // rows),
        in_specs=[
            pl.BlockSpec((1, rows, CONV_DIM), blk),
            pl.BlockSpec((1, rows, GDN_W), blk),
            pl.BlockSpec((1, rows, LANES), blk),
            pl.BlockSpec((CONV_K, CONV_DIM), full),
            pl.BlockSpec((1, LANES), full),
            pl.BlockSpec((1, LANES), full),
            pl.BlockSpec((1, GDN_DV), full),
        ],
        out_specs=pl.BlockSpec((1, rows, GDN_W), blk),
        out_shape=jax.ShapeDtypeStruct((b, s, GDN_W), F32),
        scratch_shapes=[pltpu.VMEM((8, CONV_DIM), F32), pltpu.VMEM((GDN_DK, GDN_HEADS * GDN_DV), F32)],
        compiler_params=_cparams(("parallel", "arbitrary")),
        name="gdn",
    )(qkv, gate, ba, conv_w.astype(F32), alane, dtlane, gdn_norm_w.astype(F32).reshape(1, GDN_DV))


def _t5_bucket(dist):
    max_exact = REL_BUCKETS // 2
    n = np.maximum(dist, 0)
    large = max_exact + (np.log(np.maximum(n, 1) / max_exact) / math.log(REL_MAX_DIST / max_exact)
                         * (REL_BUCKETS - max_exact)).astype(np.int32)
    large = np.minimum(large, REL_BUCKETS - 1)
    return np.where(n < max_exact, n, large).astype(np.int32)


def _branch_bias(rel_bias, window, dilation):
    kj = np.arange(2 * SWA_BLOCK)[:, None]
    qi = np.arange(SWA_BLOCK)[None, :]
    steps = qi + SWA_BLOCK - kj
    valid = (steps >= 0) & (steps <= window // dilation)
    onehot = np.eye(REL_BUCKETS, dtype=np.float32)[_t5_bucket(steps * dilation).reshape(-1)]
    bias = jnp.dot(jnp.asarray(onehot), rel_bias.astype(F32), precision=lax.Precision.HIGHEST)
    bias = bias.reshape(2 * SWA_BLOCK, SWA_BLOCK, SWA_HEADS).transpose(2, 0, 1)
    bias = jnp.where(jnp.asarray(valid)[None], bias, NEG_INF)
    return jnp.concatenate([bias[:, :SWA_BLOCK], jnp.full((SWA_HEADS, SWA_BLOCK, SWA_BLOCK), NEG_INF, F32),
                            bias[:, SWA_BLOCK:]], axis=1)


SWA_SUPER = 2048
STAT_ROWS = 16


def _swa_body(q_ref, k_ref, v_ref, bias_ref, spread_ref, o_ref, acco, accl):
    sb = pl.program_id(1)
    n_qblk = SWA_SUPER // SWA_BLOCK
    stat_pad = jnp.zeros((LANES - SWA_HEADS, SWA_BLOCK), F32)
    last_branch = len(DILATED_BRANCHES) - 1

    for bi, (_, d) in enumerate(DILATED_BRANCHES):
        dsh = d.bit_length() - 1
        per_class = n_qblk // d

        def rows(start, d=d):
            if d > 1:
                return pl.ds(start, SWA_BLOCK, stride=d)
            return pl.ds(pl.multiple_of(start, SWA_BLOCK), SWA_BLOCK)

        def qblock(idx, carry, bi=bi, d=d, dsh=dsh, per_class=per_class, rows=rows):
            r = jnp.bitwise_and(idx, d - 1)
            n = jnp.right_shift(idx, dsh)
            loc = r + n * (d * SWA_BLOCK)
            cur = sb * SWA_SUPER + loc
            first = (sb * per_class + n) == 0
            prev = jnp.maximum(cur - d * SWA_BLOCK, r)
            poff = pl.multiple_of(jnp.where(first, SWA_BLOCK, 0), SWA_BLOCK)
            scores, values = [], []
            for j in range(SWA_SLABS):
                qb = q_ref[j, rows(loc), :].astype(BF16)
                kw = jnp.concatenate([k_ref[j, rows(prev), :], k_ref[j, rows(cur), :]], axis=0).astype(BF16)
                vw = jnp.concatenate([v_ref[j, rows(prev), :], v_ref[j, rows(cur), :]], axis=0).astype(BF16)
                for lo in range(0, LANES, SWA_DH):
                    scores.append(_bdot_nt(kw[:, lo:lo + SWA_DH], qb[:, lo:lo + SWA_DH]))
                    values.append(vw[:, lo:lo + SWA_DH])
            mxs, dens, probs = [], [], []
            for h in range(SWA_HEADS):
                bias = jnp.concatenate([bias_ref[bi, h, pl.ds(poff, SWA_BLOCK), :],
                                        bias_ref[bi, h, 2 * SWA_BLOCK:3 * SWA_BLOCK, :]], axis=0)
                st = scores[h] + bias
                mx = jnp.max(st, axis=0, keepdims=True)
                p = jnp.exp(st - mx)
                mxs.append(mx)
                dens.append(jnp.sum(p, axis=0, keepdims=True))
                probs.append(p.astype(BF16))
            outs = [_bdot_tn(values[h], probs[h]) for h in range(SWA_HEADS)]
            mx = jnp.concatenate(mxs, axis=0)
            den = jnp.concatenate(dens, axis=0)
            lse = mx + jnp.log(den)
            if bi == 0:
                scale = 1.0 / den
                new = lse
            else:
                lacc = jnp.transpose(accl[rows(loc), :])[0:SWA_HEADS]
                m2 = jnp.maximum(lacc, lse)
                ea = jnp.exp(lacc - m2)
                tot = ea + jnp.exp(lse - m2)
                scale = jnp.exp(mx - m2) / tot
                keep = ea / tot
                new = m2 + jnp.log(tot)
            o_tok = jnp.transpose(
                jnp.concatenate([outs[h] * scale[h:h + 1] for h in range(SWA_HEADS)], axis=0))
            if bi > 0:
                kpad = jnp.concatenate([keep, jnp.zeros((STAT_ROWS - SWA_HEADS, SWA_BLOCK), F32)], axis=0)
                hi = kpad.astype(BF16)
                lo2 = (kpad - hi.astype(F32)).astype(BF16)
                spread = spread_ref[...]
                keep_tok = _bdot_tn(hi, spread) + _bdot_tn(lo2, spread)
            for j in range(SWA_SLABS):
                o_slab = o_tok[:, j * LANES:(j + 1) * LANES]
                if bi > 0:
                    o_slab = acco[j, rows(loc), :] * keep_tok[:, j * LANES:(j + 1) * LANES] + o_slab
                acco[j, rows(loc), :] = o_slab
            if bi < last_branch:
                accl[rows(loc), :] = jnp.transpose(jnp.concatenate([new, stat_pad], axis=0))
            return carry

        lax.fori_loop(0, n_qblk, qblock, 0)
    for j in range(SWA_SLABS):
        o_ref[:, j * LANES:(j + 1) * LANES] = acco[j].astype(BF16)


def _dilated_attention(qb, kb, vb, rel_bias, b, s):
    bias = jnp.stack([_branch_bias(rel_bias, w, d) for w, d in DILATED_BRANCHES])
    spread = np.zeros((STAT_ROWS, SWA_W), np.float32)
    for h in range(SWA_HEADS):
        spread[h, h * SWA_DH:(h + 1) * SWA_DH] = 1.0
    once = pl.Buffered(1)
    spans = s // SWA_SUPER
    span = lambda i, j: (0, i * spans + j, 0)
    seq = lambda i, j: (0, i, 0)
    return pl.pallas_call(
        _swa_body,
        grid=(b, spans),
        in_specs=[
            pl.BlockSpec((SWA_SLABS, SWA_SUPER, LANES), span),
            pl.BlockSpec((SWA_SLABS, s, LANES), seq, pipeline_mode=once),
            pl.BlockSpec((SWA_SLABS, s, LANES), seq, pipeline_mode=once),
            pl.BlockSpec(bias.shape, lambda i, j: (0, 0, 0, 0), pipeline_mode=once),
            pl.BlockSpec((STAT_ROWS, SWA_W), lambda i, j: (0, 0)),
        ],
        out_specs=pl.BlockSpec((SWA_SUPER, SWA_W), lambda i, j: (i * spans + j, 0)),
        out_shape=jax.ShapeDtypeStruct((b * s, SWA_W), BF16),
        scratch_shapes=[pltpu.VMEM((SWA_SLABS, SWA_SUPER, LANES), F32), pltpu.VMEM((SWA_SUPER, LANES), F32)],
        compiler_params=_cparams(("parallel", "arbitrary")),
        name="swa",
    )(qb, kb, vb, bias, jnp.asarray(spread, BF16))


TOKEN_TILE = D_MODEL // LANES


def _put_token_tiles(ref, rows):
    m = rows.shape[0]
    for s in range(TOKEN_TILE):
        ref[pl.ds(s, m, stride=TOKEN_TILE), :] = rows[:, s * LANES:(s + 1) * LANES]


def _get_token_tiles(ref, m, base=0):
    return jnp.concatenate([ref[pl.ds(base + s, m, stride=TOKEN_TILE), :] for s in range(TOKEN_TILE)], axis=-1)


def _out_proj_body(x_ref, oa_ref, ob_ref, w_ref, nw_ref, wr_ref, br_ref, h_ref, hn_ref, e_ref, g_ref):
    h = (x_ref[...] + _bdot(oa_ref[...], w_ref[0:GDN_W, :]) + _bdot(ob_ref[...], w_ref[GDN_W:, :]))
    h_ref[...] = h
    hn = h * lax.rsqrt(jnp.mean(h * h, axis=-1, keepdims=True) + EPS) * nw_ref[...]
    _put_token_tiles(hn_ref, hn)
    hn_hi = hn.astype(BF16)
    hn_lo = (hn - hn_hi.astype(F32)).astype(BF16)
    logits = (jnp.dot(hn_hi, wr_ref[0], preferred_element_type=F32)
              + jnp.dot(hn_hi, wr_ref[1], preferred_element_type=F32)
              + jnp.dot(hn_lo, wr_ref[0], preferred_element_type=F32) + br_ref[...])
    lane = lax.broadcasted_iota(I32, logits.shape, 1)
    e_tile = jnp.zeros(logits.shape, I32)
    g_tile = jnp.zeros(logits.shape, F32)
    top0 = None
    den = None
    for k in range(TOP_K):
        mx = jnp.max(logits, axis=-1, keepdims=True)
        idx = jnp.min(jnp.where(logits == mx, lane, LANES), axis=-1, keepdims=True)
        logits = jnp.where(lane == idx, NEG_INF, logits)
        if k == 0:
            top0 = mx
        ex = jnp.exp(mx - top0)
        den = ex if k == 0 else den + ex
        e_tile = jnp.where(lane == k, idx, e_tile)
        g_tile = jnp.where(lane == k, ex, g_tile)
    e_ref[...] = e_tile
    g_ref[...] = g_tile / den


def _out_proj(x2, oa, ob, w_out, norm2_w, w_router, b_router, tm=512):
    t = x2.shape[0]
    wr = jnp.pad(w_router.astype(F32), ((0, 0), (0, LANES - N_EXPERTS)))
    wr_hi = wr.astype(BF16)
    wr = jnp.stack([wr_hi, (wr - wr_hi.astype(F32)).astype(BF16)])
    br = jnp.concatenate([b_router.astype(F32), jnp.full((LANES - N_EXPERTS,), NEG_INF, F32)]).reshape(1, LANES)
    row = lambda i: (i, 0)
    full = lambda i: (0, 0)
    return pl.pallas_call(
        _out_proj_body,
        grid=(t // tm,),
        in_specs=[
            pl.BlockSpec((tm, D_MODEL), row),
            pl.BlockSpec((tm, GDN_W), row),
            pl.BlockSpec((tm, SWA_W), row),
            pl.BlockSpec((GDN_W + SWA_W, D_MODEL), full),
            pl.BlockSpec((1, D_MODEL), full),
            pl.BlockSpec((2, D_MODEL, LANES), lambda i: (0, 0, 0)),
            pl.BlockSpec((1, LANES), full),
        ],
        out_specs=[
            pl.BlockSpec((tm, D_MODEL), row),
            pl.BlockSpec((tm * TOKEN_TILE, LANES), row),
            pl.BlockSpec((tm, LANES), row),
            pl.BlockSpec((tm, LANES), row),
        ],
        out_shape=[
            jax.ShapeDtypeStruct((t, D_MODEL), F32),
            jax.ShapeDtypeStruct((t * TOKEN_TILE, LANES), F32),
            jax.ShapeDtypeStruct((t, LANES), I32),
            jax.ShapeDtypeStruct((t, LANES), F32),
        ],
        compiler_params=_cparams(("parallel",)),
        name="out_proj",
    )(x2, oa, ob, w_out.astype(BF16), norm2_w.astype(F32).reshape(1, D_MODEL), wr, br)


def _moe_plan(top_e):
    t = top_e.shape[0]
    n = t * TOP_K
    p_rows = n + N_EXPERTS * MOE_BLOCK
    nblk = p_rows // MOE_BLOCK
    e_flat = top_e.reshape(-1)
    experts = jnp.arange(N_EXPERTS, dtype=I32)
    counts = jnp.sum((e_flat[:, None] == experts[None, :]).astype(I32), axis=0)
    padded = (counts + MOE_BLOCK - 1) // MOE_BLOCK * MOE_BLOCK
    pad_end = jnp.cumsum(padded)
    cand = jnp.arange(MOE_BLOCK, dtype=I32)[None, :] < (padded - counts)[:, None]
    cand_key = jnp.where(cand, 2 * experts[:, None] + 1, 2 * N_EXPERTS).reshape(-1)
    keys = jnp.concatenate([2 * e_flat, cand_key])
    vals = jnp.concatenate([jnp.arange(n, dtype=I32), jnp.full((N_EXPERTS * MOE_BLOCK,), -1, I32)])
    _, src = lax.sort_key_val(keys, vals)
    is_pad = src < 0
    pad_rank = jnp.cumsum(is_pad.astype(I32)) - 1
    tok = jnp.right_shift(src, TOP_K.bit_length() - 1)
    gather_tok = jnp.where(is_pad, 0, tok)
    scatter_row = jnp.where(is_pad, n + pad_rank, jnp.bitwise_and(src, TOP_K - 1) * t + tok)
    blk_start = jnp.arange(nblk, dtype=I32) * MOE_BLOCK
    blk_expert = jnp.minimum(jnp.sum((blk_start[:, None] >= pad_end[None, :]).astype(I32), axis=1), N_EXPERTS - 1)
    n_used = (pad_end[-1] // MOE_BLOCK).reshape(1)
    return gather_tok.reshape(nblk, 1, MOE_BLOCK), scatter_row.reshape(nblk, 1, MOE_BLOCK), blk_expert, n_used


MOE_COL_CHUNK = 256


def _moe_body(bexp_ref, nused_ref, gidx_ref, sidx_ref, hn_ref, wgu_ref, bgu_ref, wdn_ref, bdn_ref, y_ref,
              xbuf, ybuf, wgu_bf, wdn_bf, gsem, ssem):
    i = pl.program_id(0)
    last = pl.num_programs(0) - 1
    slot = i % 2

    def gather_copy(tok, s, r):
        return pltpu.make_async_copy(hn_ref.at[tok], xbuf.at[s, pl.ds(r * TOKEN_TILE, TOKEN_TILE)], gsem.at[s])

    def scatter_copy(row, r):
        return pltpu.make_async_copy(ybuf.at[pl.ds(r * TOKEN_TILE, TOKEN_TILE)], y_ref.at[row], ssem)

    @pl.when(i == 0)
    def _():
        for r in range(MOE_BLOCK):
            gather_copy(gidx_ref[0, 0, r], 0, r).start()
        ybuf[...] = jnp.zeros_like(ybuf)

    @pl.when(i >= 0)
    def _():
        for r in range(MOE_BLOCK):
            scatter_copy(sidx_ref[0, 0, r], r).start(priority=r % 2)
            gather_copy(gidx_ref[1, 0, r], 1 - slot, r).start(priority=r % 2)

    @pl.when(jnp.logical_or(i == 0, bexp_ref[i] != bexp_ref[jnp.maximum(i - 1, 0)]))
    def _():
        wgu_bf[...] = wgu_ref[0].astype(BF16)
        wdn_bf[...] = wdn_ref[0].astype(BF16)

    for r in range(MOE_BLOCK):
        gather_copy(0, slot, r).wait()

    @pl.when(i < nused_ref[0])
    def _():
        x = _get_token_tiles(xbuf.at[slot], MOE_BLOCK).astype(BF16)
        n_col = D_FF // MOE_COL_CHUNK
        acts = []
        for c in range(n_col):
            lo = c * MOE_COL_CHUNK
            g = jnp.dot(x, wgu_bf[:, lo:lo + MOE_COL_CHUNK], preferred_element_type=F32)
            u = jnp.dot(x, wgu_bf[:, D_FF + lo:D_FF + lo + MOE_COL_CHUNK], preferred_element_type=F32)
            g = jnp.minimum(g + bgu_ref[0, :, lo:lo + MOE_COL_CHUNK], SWIGLU_LIMIT)
            u = jnp.clip(u + bgu_ref[0, :, D_FF + lo:D_FF + lo + MOE_COL_CHUNK], -SWIGLU_LIMIT, SWIGLU_LIMIT)
            acts.append(((u + 1.0) * g * jax.nn.sigmoid(SWIGLU_ALPHA * g)).astype(BF16))
        act = jnp.concatenate(acts, axis=-1)
        outs = []
        for c in range(D_MODEL // MOE_COL_CHUNK):
            lo = c * MOE_COL_CHUNK
            outs.append(jnp.dot(act, wdn_bf[:, lo:lo + MOE_COL_CHUNK], preferred_element_type=F32)
                        + bdn_ref[0, :, lo:lo + MOE_COL_CHUNK])
        y = jnp.concatenate(outs, axis=-1)
        for r in range(MOE_BLOCK):
            scatter_copy(0, r).wait()
        _put_token_tiles(ybuf, y)

    @pl.when(i >= nused_ref[0])
    def _():
        for r in range(MOE_BLOCK):
            scatter_copy(0, r).wait()

    @pl.when(i == last)
    def _():
        for r in range(MOE_BLOCK):
            scatter_copy(sidx_ref[1, 0, r], r).start()
        for r in range(MOE_BLOCK):
            scatter_copy(0, r).wait()
        for r in range(MOE_BLOCK):
            gather_copy(0, 1 - slot, r).wait()


def _moe_ffn(hn, plan, w_gu, b_gu, w_dn, b_dn):
    t = hn.shape[0] // TOKEN_TILE
    gather_tok, scatter_row, blk_expert, n_used = plan
    nblk = gather_tok.shape[0]
    n_rows = t * TOP_K + (N_EXPERTS + 1) * MOE_BLOCK
    gnext = jnp.concatenate([gather_tok[1:], gather_tok[-1:]], axis=0)
    gpair = jnp.concatenate([gather_tok, gnext], axis=1).reshape(nblk * 2, 1, MOE_BLOCK)
    before_first = (n_rows - MOE_BLOCK + jnp.arange(MOE_BLOCK, dtype=I32)).reshape(1, 1, MOE_BLOCK)
    sprev = jnp.concatenate([before_first, scatter_row[:-1]], axis=0)
    spair = jnp.concatenate([sprev, scatter_row], axis=1).reshape(nblk * 2, 1, MOE_BLOCK)
    wsel = lambda i, be, nu: (be[i], 0, 0)
    pair = lambda i, be, nu: (i, 0, 0)
    grid_spec = pltpu.PrefetchScalarGridSpec(
        num_scalar_prefetch=2,
        grid=(nblk,),
        in_specs=[
            pl.BlockSpec((2, 1, MOE_BLOCK), pair, memory_space=pltpu.SMEM),
            pl.BlockSpec((2, 1, MOE_BLOCK), pair, memory_space=pltpu.SMEM),
            pl.BlockSpec(memory_space=pl.ANY),
            pl.BlockSpec((1, D_MODEL, 2 * D_FF), wsel),
            pl.BlockSpec((1, 1, 2 * D_FF), wsel),
            pl.BlockSpec((1, D_FF, D_MODEL), wsel),
            pl.BlockSpec((1, 1, D_MODEL), wsel),
        ],
        out_specs=pl.BlockSpec(memory_space=pl.ANY),
        scratch_shapes=[
            pltpu.VMEM((2, MOE_BLOCK * TOKEN_TILE, LANES), F32),
            pltpu.VMEM((MOE_BLOCK * TOKEN_TILE, LANES), F32),
            pltpu.VMEM((D_MODEL, 2 * D_FF), BF16),
            pltpu.VMEM((D_FF, D_MODEL), BF16),
            pltpu.SemaphoreType.DMA((2,)),
            pltpu.SemaphoreType.DMA(()),
        ],
    )
    return pl.pallas_call(
        _moe_body,
        grid_spec=grid_spec,
        out_shape=jax.ShapeDtypeStruct((n_rows, TOKEN_TILE, LANES), F32),
        compiler_params=_cparams(("arbitrary",)),
        name="moe_ffn",
    )(blk_expert, n_used, gpair, spair, hn.reshape(t, TOKEN_TILE, LANES),
      w_gu.astype(F32), b_gu.astype(F32).reshape(N_EXPERTS, 1, 2 * D_FF),
      w_dn.astype(F32), b_dn.astype(F32).reshape(N_EXPERTS, 1, D_MODEL))


def _combine_body(h_ref, g_ref, *refs):
    y_refs, o_ref = refs[:TOP_K], refs[TOP_K]
    g = g_ref[...]
    m = g.shape[0]
    moe = g[:, 0:1] * _get_token_tiles(y_refs[0], m)
    for k in range(1, TOP_K):
        moe = moe + g[:, k:k + 1] * _get_token_tiles(y_refs[k], m)
    o_ref[...] = h_ref[...] + moe


def _combine(h, y_slots, gates, tm=512):
    t = h.shape[0]
    y2 = y_slots.reshape(-1, LANES)
    row = lambda i: (i, 0)
    kth = lambda k: (lambda i: (k * (t // tm) + i, 0))
    return pl.pallas_call(
        _combine_body,
        grid=(t // tm,),
        in_specs=[pl.BlockSpec((tm, D_MODEL), row), pl.BlockSpec((tm, LANES), row)]
        + [pl.BlockSpec((tm * TOKEN_TILE, LANES), kth(k)) for k in range(TOP_K)],
        out_specs=pl.BlockSpec((tm, D_MODEL), row),
        out_shape=jax.ShapeDtypeStruct((t, D_MODEL), F32),
        compiler_params=_cparams(("parallel",)),
        name="combine",
    )(h, gates, *([y2] * TOP_K))


def _layer(x, norm1_w, w_in, conv_w, a_log, dt_bias, gdn_norm_w, q_norm_w, k_norm_w, rel_bias,
           w_out, norm2_w, w_router, b_router, w_gate_up, b_gate_up, w_down, b_down):
    b, s, d = x.shape
    t = b * s
    x2 = x.reshape(t, d)
    qkv, gate, ba, qb, kb, vb = _in_proj(x2, norm1_w.astype(F32), _pack_w_in(w_in),
                                         q_norm_w.astype(F32), k_norm_w.astype(F32))
    oa = _gdn(qkv.reshape(b, s, CONV_DIM), gate.reshape(b, s, GDN_W), ba.reshape(b, s, LANES),
              conv_w, a_log, dt_bias, gdn_norm_w)
    ob = _dilated_attention(qb, kb, vb, rel_bias, b, s)
    h, hn, top_e, gates = _out_proj(x2, oa.reshape(t, GDN_W), ob, w_out, norm2_w, w_router, b_router)
    plan = _moe_plan(top_e[:, :TOP_K])
    y_slots = _moe_ffn(hn, plan, w_gate_up, b_gate_up, w_down, b_down)
    return _combine(h, y_slots, gates).reshape(b, s, d)


def kernel(x, norm1_w, w_in, conv_w, a_log, dt_bias, gdn_norm_w, q_norm_w, k_norm_w, rel_bias, w_out, norm2_w,
           w_router, b_router, w_gate_up, b_gate_up, w_down, b_down):
    h = x
    for l in range(norm1_w.shape[0]):
        h = _layer(h, norm1_w[l], w_in[l], conv_w[l], a_log[l], dt_bias[l], gdn_norm_w[l], q_norm_w[l],
                   k_norm_w[l], rel_bias, w_out[l], norm2_w[l], w_router[l], b_router[l], w_gate_up[l],
                   b_gate_up[l], w_down[l], b_down[l])
    return h
```

```python
import functools
import math

import numpy as np
import jax
import jax.numpy as jnp
from jax import lax
from jax.experimental import pallas as pl
from jax.experimental.pallas import tpu as pltpu

F32 = jnp.float32
BF16 = jnp.bfloat16
I32 = jnp.int32

EPS = 1e-6
D_MODEL = 1024
GDN_HEADS = 4
GDN_DK = 128
GDN_DV = 128
CONV_K = 4
SWA_HEADS = 8
SWA_DH = 64
DILATED_BRANCHES = ((128, 1), (512, 4), (2048, 16))
SWA_BLOCK = 128
REL_BUCKETS = 32
REL_MAX_DIST = 2048
N_EXPERTS = 32
TOP_K = 4
D_FF = D_MODEL
SWIGLU_LIMIT = 7.0
SWIGLU_ALPHA = 1.702
MOE_BLOCK = 256

LANES = 128
GDN_W = GDN_HEADS * GDN_DK
SWA_W = SWA_HEADS * SWA_DH
SWA_SLABS = SWA_W // LANES
CONV_DIM = 3 * GDN_W
COL_QKV = 0
COL_GATE = COL_QKV + CONV_DIM
COL_BA = COL_GATE + GDN_W
COL_QB = COL_BA + LANES
COL_KB = COL_QB + SWA_W
COL_VB = COL_KB + SWA_W
IN_PACKED = COL_VB + SWA_W

VMEM_LIMIT = 56 * 1024 * 1024

NEG_INF = float("-inf")


def _cparams(sem, **kw):
    return pltpu.CompilerParams(dimension_semantics=sem, vmem_limit_bytes=VMEM_LIMIT, **kw)


def _bdot(a, b):
    return jnp.dot(a.astype(BF16), b.astype(BF16), preferred_element_type=F32)


def _bdot_nt(a, b):
    return lax.dot_general(a.astype(BF16), b.astype(BF16), (((1,), (1,)), ((), ())),
                           preferred_element_type=F32)


def _bdot_tn(a, b):
    return lax.dot_general(a.astype(BF16), b.astype(BF16), (((0,), (0,)), ((), ())),
                           preferred_element_type=F32)


def _split3(x):
    hi = x.astype(BF16)
    r1 = x - hi.astype(F32)
    mid = r1.astype(BF16)
    lo = (r1 - mid.astype(F32)).astype(BF16)
    return hi, mid, lo


def _dot_split3(a_bf16, x):
    return sum(jnp.dot(a_bf16, part, preferred_element_type=F32) for part in _split3(x))


def _silu(x):
    return x * jax.nn.sigmoid(x)


def _in_proj_body(x_ref, nw_ref, w_ref, qnw_ref, knw_ref, hsum_ref,
                  qkv_ref, gate_ref, ba_ref, qb_ref, kb_ref, vb_ref):
    x = x_ref[...]
    ms = jnp.mean(x * x, axis=-1, keepdims=True)
    xn = (x * lax.rsqrt(ms + EPS) * nw_ref[...]).astype(BF16)

    def seg(lo, width):
        return jnp.dot(xn, w_ref[:, lo:lo + width], preferred_element_type=F32)

    qkv_ref[...] = seg(COL_QKV, CONV_DIM)
    gate_ref[...] = seg(COL_GATE, GDN_W)
    ba_ref[...] = seg(COL_BA, LANES)

    def head_rmsnorm(z, w_row, scale):
        sq = z * z
        hi = sq.astype(BF16)
        lo = (sq - hi.astype(F32)).astype(BF16)
        hmean = (jnp.dot(hi, hsum_ref[...], preferred_element_type=F32)
                 + jnp.dot(lo, hsum_ref[...], preferred_element_type=F32))
        return z * lax.rsqrt(hmean + EPS) * w_row * scale

    def put_slabs(ref, z):
        for j in range(SWA_SLABS):
            ref[j] = z[:, j * LANES:(j + 1) * LANES]

    put_slabs(qb_ref, head_rmsnorm(seg(COL_QB, SWA_W), qnw_ref[...], SWA_DH ** -0.5))
    put_slabs(kb_ref, head_rmsnorm(seg(COL_KB, SWA_W), knw_ref[...], 1.0))
    put_slabs(vb_ref, seg(COL_VB, SWA_W))


def _in_proj(x2, norm1_w, w_packed, q_norm_w, k_norm_w, tm=512):
    t = x2.shape[0]
    hsum = jnp.asarray(np.kron(np.eye(SWA_HEADS), np.ones((SWA_DH, SWA_DH))) / SWA_DH, BF16)
    row = lambda i: (i, 0)
    full = lambda i: (0, 0)
    slab = lambda i: (0, i, 0)
    return pl.pallas_call(
        _in_proj_body,
        grid=(t // tm,),
        in_specs=[
            pl.BlockSpec((tm, D_MODEL), row),
            pl.BlockSpec((1, D_MODEL), full),
            pl.BlockSpec((D_MODEL, IN_PACKED), full),
            pl.BlockSpec((1, SWA_W), full),
            pl.BlockSpec((1, SWA_W), full),
            pl.BlockSpec((SWA_W, SWA_W), full),
        ],
        out_specs=[
            pl.BlockSpec((tm, CONV_DIM), row),
            pl.BlockSpec((tm, GDN_W), row),
            pl.BlockSpec((tm, LANES), row),
            pl.BlockSpec((SWA_SLABS, tm, LANES), slab),
            pl.BlockSpec((SWA_SLABS, tm, LANES), slab),
            pl.BlockSpec((SWA_SLABS, tm, LANES), slab),
        ],
        out_shape=[
            jax.ShapeDtypeStruct((t, CONV_DIM), F32),
            jax.ShapeDtypeStruct((t, GDN_W), F32),
            jax.ShapeDtypeStruct((t, LANES), F32),
            jax.ShapeDtypeStruct((SWA_SLABS, t, LANES), F32),
            jax.ShapeDtypeStruct((SWA_SLABS, t, LANES), F32),
            jax.ShapeDtypeStruct((SWA_SLABS, t, LANES), F32),
        ],
        compiler_params=_cparams(("parallel",)),
        name="in_proj",
    )(x2, norm1_w.reshape(1, D_MODEL), w_packed,
      jnp.tile(q_norm_w, SWA_HEADS).reshape(1, SWA_W), jnp.tile(k_norm_w, SWA_HEADS).reshape(1, SWA_W), hsum)


def _pack_w_in(w_in):
    o = np.cumsum([0, GDN_W, GDN_W, GDN_W, GDN_W, GDN_HEADS, GDN_HEADS, SWA_W, SWA_W, SWA_W])
    ba = jnp.pad(w_in[:, o[4]:o[6]], ((0, 0), (0, LANES - 2 * GDN_HEADS)))
    return jnp.concatenate([w_in[:, o[0]:o[4]], ba, w_in[:, o[6]:o[9]]], axis=1).astype(BF16)


def _unit_lower_inverses(a_list, n, blk):
    ii = lax.broadcasted_iota(I32, (n, n), 0)
    jj = lax.broadcasted_iota(I32, (n, n), 1)
    same = lambda sh: jnp.right_shift(ii, sh) == jnp.right_shift(jj, sh)
    eye = (ii == jj).astype(F32)
    pair = same(1)
    t_list = [eye - jnp.where(pair, a, 0.0) for a in a_list]
    sh = 1
    while (1 << sh) < blk:
        join = same(sh + 1) & jnp.logical_not(same(sh))
        te = [_bdot(t, jnp.where(join, a, 0.0)) for t, a in zip(t_list, a_list)]
        tet = [_bdot(x, t) for x, t in zip(te, t_list)]
        t_list = [t - y for t, y in zip(t_list, tet)]
        sh += 1
    return t_list


def _gdn_body(chunk, n_chunks, qkv_ref, gate_ref, ba_ref, convw_ref, alane_ref, dtlane_ref, normw_ref,
              o_ref, tail_ref, state_ref):
    c = chunk
    rows = chunk * n_chunks

    @pl.when(pl.program_id(1) == 0)
    def _():
        tail_ref[...] = jnp.zeros_like(tail_ref)
        state_ref[...] = jnp.zeros_like(state_ref)

    x = qkv_ref[0]
    xs = jnp.concatenate([tail_ref[...], x], axis=0)
    w = convw_ref[...]
    conv = (xs[5:5 + rows] * w[0:1] + xs[6:6 + rows] * w[1:2] + xs[7:7 + rows] * w[2:3] + x * w[3:4])
    tail_ref[...] = x[rows - 8:rows]
    y = _silu(conv)

    ba = ba_ref[0]
    beta_all = jax.nn.sigmoid(ba)
    z = ba + dtlane_ref[...]
    softplus = jnp.maximum(z, 0.0) + jnp.log(1.0 + jnp.exp(-jnp.abs(z)))
    g_all = -alane_ref[...] * softplus
    ri = lax.broadcasted_iota(I32, (rows, rows), 0)
    ci = lax.broadcasted_iota(I32, (rows, rows), 1)
    csh = c.bit_length() - 1
    tri = ((ri >= ci) & (jnp.right_shift(ri, csh) == jnp.right_shift(ci, csh))).astype(F32)
    gc_all = _dot_split3(tri.astype(BF16), g_all)

    heads = range(GDN_HEADS)
    qn, kn, vv = [], [], []
    for h in heads:
        lo = h * GDN_DK
        q = y[:, lo:lo + GDN_DK]
        k = y[:, GDN_W + lo:GDN_W + lo + GDN_DK]
        qn.append(q * lax.rsqrt(jnp.sum(q * q, axis=-1, keepdims=True) + EPS) * (GDN_DK ** -0.5))
        kn.append(k * lax.rsqrt(jnp.sum(k * k, axis=-1, keepdims=True) + EPS))
        vv.append(y[:, 2 * GDN_W + lo:2 * GDN_W + lo + GDN_DV])

    hs = GDN_HEADS * c
    hsh = csh
    ii = lax.broadcasted_iota(I32, (hs, hs), 0)
    jj = lax.broadcasted_iota(I32, (hs, hs), 1)
    same_head = jnp.right_shift(ii, hsh) == jnp.right_shift(jj, hsh)
    row_head = jnp.right_shift(lax.broadcasted_iota(I32, (hs, GDN_DV), 0), hsh)

    def stack(parts, r0):
        return jnp.concatenate([p[r0:r0 + c] for p in parts], axis=0)

    def col(src, lane0, r0):
        return jnp.concatenate([src[r0:r0 + c, lane0 + h:lane0 + h + 1] for h in heads], axis=0)

    def diag_blocks(p):
        out = jnp.where(row_head == 0, p[:, 0:GDN_DV], 0.0)
        for h in range(1, GDN_HEADS):
            out = out + jnp.where(row_head == h, p[:, h * GDN_DV:(h + 1) * GDN_DV], 0.0)
        return out

    q_st, u_rhs, kdec, cdecay, a_kk, a_qk = ([] for _ in range(6))
    for n in range(n_chunks):
        r0 = n * c
        k = stack(kn, r0)
        q = stack(qn, r0)
        v = stack(vv, r0)
        beta = col(beta_all, 0, r0)
        gcol = col(gc_all, GDN_HEADS, r0)
        grow = jnp.transpose(jnp.broadcast_to(gcol, (hs, LANES)))[0:1, :]
        glast = jnp.concatenate(
            [jnp.broadcast_to(gc_all[r0 + c - 1:r0 + c, GDN_HEADS + h:GDN_HEADS + h + 1], (c, 1)) for h in heads],
            axis=0)
        dec = jnp.exp(jnp.where(same_head & (ii >= jj), gcol - grow, NEG_INF))
        eg = jnp.exp(gcol)
        kb = k * beta
        prod = _bdot_nt(jnp.concatenate([kb, q], axis=0), k)
        a_kk.append(jnp.where(ii > jj, prod[:hs] * dec, 0.0))
        a_qk.append(prod[hs:] * dec)
        q_st.append(q * eg)
        u_rhs.append(jnp.concatenate([v * beta, kb * eg], axis=-1))
        kdec.append(k * jnp.exp(glast - gcol))
        cdecay.append(jnp.concatenate(
            [jnp.broadcast_to(jnp.exp(gc_all[r0 + c - 1:r0 + c, GDN_HEADS + h:GDN_HEADS + h + 1]), (1, GDN_DV))
             for h in heads], axis=-1))
    t_inv = _unit_lower_inverses(a_kk, hs, c)
    sols = [_bdot(t, r) for t, r in zip(t_inv, u_rhs)]

    gate = gate_ref[0]
    normw = normw_ref[...]
    s_cat = state_ref[...]
    for n in range(n_chunks):
        r0 = n * c
        u, wk = sols[n][:, :GDN_DV], sols[n][:, GDN_DV:]
        p = _bdot(jnp.concatenate([wk, q_st[n]], axis=0), s_cat)
        v_new = u - diag_blocks(p[:hs])
        o = diag_blocks(p[hs:]) + _bdot(a_qk[n], v_new)
        v_bd = jnp.concatenate([jnp.where(row_head == h, v_new, 0.0) for h in heads], axis=-1)
        s_cat = s_cat * cdecay[n] + _bdot_tn(kdec[n], v_bd)
        o = o * lax.rsqrt(jnp.mean(o * o, axis=-1, keepdims=True) + EPS) * normw
        for h in heads:
            lo = h * GDN_DV
            o_ref[0, r0:r0 + c, lo:lo + GDN_DV] = o[h * c:(h + 1) * c] * _silu(gate[r0:r0 + c, lo:lo + GDN_DV])
    state_ref[...] = s_cat


def _gdn(qkv, gate, ba, conv_w, a_log, dt_bias, gdn_norm_w, chunk=64, n_chunks=4):
    b, s, _ = qkv.shape
    rows = chunk * n_chunks
    pad = jnp.zeros((LANES - 2 * GDN_HEADS,), F32)
    alane = jnp.concatenate([jnp.zeros((GDN_HEADS,), F32), jnp.exp(a_log.astype(F32)), pad]).reshape(1, LANES)
    dtlane = jnp.concatenate([jnp.zeros((GDN_HEADS,), F32), dt_bias.astype(F32), pad]).reshape(1, LANES)
    blk = lambda i, j: (i, j, 0)
    full = lambda i, j: (0, 0)
    return pl.pallas_call(
        functools.partial(_gdn_body, chunk, n_chunks),
        grid=(b, s // rows),
        in_specs=[
            pl.BlockSpec((1, rows, CONV_DIM), blk),
            pl.BlockSpec((1, rows, GDN_W), blk),
            pl.BlockSpec((1, rows, LANES), blk),
            pl.BlockSpec((CONV_K, CONV_DIM), full),
            pl.BlockSpec((1, LANES), full),
            pl.BlockSpec((1, LANES), full),
            pl.BlockSpec((1, GDN_DV), full),
        ],
        out_specs=pl.BlockSpec((1, rows, GDN_W), blk),
        out_shape=jax.ShapeDtypeStruct((b, s, GDN_W), F32),
        scratch_shapes=[pltpu.VMEM((8, CONV_DIM), F32), pltpu.VMEM((GDN_DK, GDN_HEADS * GDN_DV), F32)],
        compiler_params=_cparams(("parallel", "arbitrary")),
        name="gdn",
    )(qkv, gate, ba, conv_w.astype(F32), alane, dtlane, gdn_norm_w.astype(F32).reshape(1, GDN_DV))


def _t5_bucket(dist):
    max_exact = REL_BUCKETS // 2
    n = np.maximum(dist, 0)
    large = max_exact + (np.log(np.maximum(n, 1) / max_exact) / math.log(REL_MAX_DIST / max_exact)
                         * (REL_BUCKETS - max_exact)).astype(np.int32)
    large = np.minimum(large, REL_BUCKETS - 1)
    return np.where(n < max_exact, n, large).astype(np.int32)


def _branch_bias(rel_bias, window, dilation):
    kj = np.arange(2 * SWA_BLOCK)[:, None]
    qi = np.arange(SWA_BLOCK)[None, :]
    steps = qi + SWA_BLOCK - kj
    valid = (steps >= 0) & (steps <= window // dilation)
    onehot = np.eye(REL_BUCKETS, dtype=np.float32)[_t5_bucket(steps * dilation).reshape(-1)]
    bias = jnp.dot(jnp.asarray(onehot), rel_bias.astype(F32), precision=lax.Precision.HIGHEST)
    bias = bias.reshape(2 * SWA_BLOCK, SWA_BLOCK, SWA_HEADS).transpose(2, 0, 1)
    bias = jnp.where(jnp.asarray(valid)[None], bias, NEG_INF)
    return jnp.concatenate([bias[:, :SWA_BLOCK], jnp.full((SWA_HEADS, SWA_BLOCK, SWA_BLOCK), NEG_INF, F32),
                            bias[:, SWA_BLOCK:]], axis=1)


SWA_SUPER = 2048
STAT_ROWS = 16


def _swa_body(q_ref, k_ref, v_ref, bias_ref, spread_ref, o_ref, acco, accl):
    sb = pl.program_id(1)
    n_qblk = SWA_SUPER // SWA_BLOCK
    stat_pad = jnp.zeros((LANES - SWA_HEADS, SWA_BLOCK), F32)
    last_branch = len(DILATED_BRANCHES) - 1

    for bi, (_, d) in enumerate(DILATED_BRANCHES):
        dsh = d.bit_length() - 1
        per_class = n_qblk // d

        def rows(start, d=d):
            if d > 1:
                return pl.ds(start, SWA_BLOCK, stride=d)
            return pl.ds(pl.multiple_of(start, SWA_BLOCK), SWA_BLOCK)

        def qblock(idx, carry, bi=bi, d=d, dsh=dsh, per_class=per_class, rows=rows):
            r = jnp.bitwise_and(idx, d - 1)
            n = jnp.right_shift(idx, dsh)
            loc = r + n * (d * SWA_BLOCK)
            cur = sb * SWA_SUPER + loc
            first = (sb * per_class + n) == 0
            prev = jnp.maximum(cur - d * SWA_BLOCK, r)
            poff = pl.multiple_of(jnp.where(first, SWA_BLOCK, 0), SWA_BLOCK)
            scores, values = [], []
            for j in range(SWA_SLABS):
                qb = q_ref[j, rows(loc), :].astype(BF16)
                kw = jnp.concatenate([k_ref[j, rows(prev), :], k_ref[j, rows(cur), :]], axis=0).astype(BF16)
                vw = jnp.concatenate([v_ref[j, rows(prev), :], v_ref[j, rows(cur), :]], axis=0).astype(BF16)
                for lo in range(0, LANES, SWA_DH):
                    scores.append(_bdot_nt(kw[:, lo:lo + SWA_DH], qb[:, lo:lo + SWA_DH]))
                    values.append(vw[:, lo:lo + SWA_DH])
            mxs, dens, probs = [], [], []
            for h in range(SWA_HEADS):
                bias = jnp.concatenate([bias_ref[bi, h, pl.ds(poff, SWA_BLOCK), :],
                                        bias_ref[bi, h, 2 * SWA_BLOCK:3 * SWA_BLOCK, :]], axis=0)
                st = scores[h] + bias
                mx = jnp.max(st, axis=0, keepdims=True)
                p = jnp.exp(st - mx)
                mxs.append(mx)
                dens.append(jnp.sum(p, axis=0, keepdims=True))
                probs.append(p.astype(BF16))
            outs = [_bdot_tn(values[h], probs[h]) for h in range(SWA_HEADS)]
            mx = jnp.concatenate(mxs, axis=0)
            den = jnp.concatenate(dens, axis=0)
            lse = mx + jnp.log(den)
            if bi == 0:
                scale = 1.0 / den
                new = lse
            else:
                lacc = jnp.transpose(accl[rows(loc), :])[0:SWA_HEADS]
                m2 = jnp.maximum(lacc, lse)
                ea = jnp.exp(lacc - m2)
                tot = ea + jnp.exp(lse - m2)
                scale = jnp.exp(mx - m2) / tot
                keep = ea / tot
                new = m2 + jnp.log(tot)
            o_tok = jnp.transpose(
                jnp.concatenate([outs[h] * scale[h:h + 1] for h in range(SWA_HEADS)], axis=0))
            if bi > 0:
                kpad = jnp.concatenate([keep, jnp.zeros((STAT_ROWS - SWA_HEADS, SWA_BLOCK), F32)], axis=0)
                hi = kpad.astype(BF16)
                lo2 = (kpad - hi.astype(F32)).astype(BF16)
                spread = spread_ref[...]
                keep_tok = _bdot_tn(hi, spread) + _bdot_tn(lo2, spread)
            for j in range(SWA_SLABS):
                o_slab = o_tok[:, j * LANES:(j + 1) * LANES]
                if bi > 0:
                    o_slab = acco[j, rows(loc), :] * keep_tok[:, j * LANES:(j + 1) * LANES] + o_slab
                acco[j, rows(loc), :] = o_slab
            if bi < last_branch:
                accl[rows(loc), :] = jnp.transpose(jnp.concatenate([new, stat_pad], axis=0))
            return carry

        lax.fori_loop(0, n_qblk, qblock, 0)
    for j in range(SWA_SLABS):
        o_ref[:, j * LANES:(j + 1) * LANES] = acco[j].astype(BF16)


def _dilated_attention(qb, kb, vb, rel_bias, b, s):
    bias = jnp.stack([_branch_bias(rel_bias, w, d) for w, d in DILATED_BRANCHES])
    spread = np.zeros((STAT_ROWS, SWA_W), np.float32)
    for h in range(SWA_HEADS):
        spread[h, h * SWA_DH:(h + 1) * SWA_DH] = 1.0
    once = pl.Buffered(1)
    spans = s // SWA_SUPER
    span = lambda i, j: (0, i * spans + j, 0)
    seq = lambda i, j: (0, i, 0)
    return pl.pallas_call(
        _swa_body,
        grid=(b, spans),
        in_specs=[
            pl.BlockSpec((SWA_SLABS, SWA_SUPER, LANES), span),
            pl.BlockSpec((SWA_SLABS, s, LANES), seq, pipeline_mode=once),
            pl.BlockSpec((SWA_SLABS, s, LANES), seq, pipeline_mode=once),
            pl.BlockSpec(bias.shape, lambda i, j: (0, 0, 0, 0), pipeline_mode=once),
            pl.BlockSpec((STAT_ROWS, SWA_W), lambda i, j: (0, 0)),
        ],
        out_specs=pl.BlockSpec((SWA_SUPER, SWA_W), lambda i, j: (i * spans + j, 0)),
        out_shape=jax.ShapeDtypeStruct((b * s, SWA_W), BF16),
        scratch_shapes=[pltpu.VMEM((SWA_SLABS, SWA_SUPER, LANES), F32), pltpu.VMEM((SWA_SUPER, LANES), F32)],
        compiler_params=_cparams(("parallel", "arbitrary")),
        name="swa",
    )(qb, kb, vb, bias, jnp.asarray(spread, BF16))


TOKEN_TILE = D_MODEL // LANES


def _put_token_tiles(ref, rows):
    m = rows.shape[0]
    for s in range(TOKEN_TILE):
        ref[pl.ds(s, m, stride=TOKEN_TILE), :] = rows[:, s * LANES:(s + 1) * LANES]


def _get_token_tiles(ref, m, base=0):
    return jnp.concatenate([ref[pl.ds(base + s, m, stride=TOKEN_TILE), :] for s in range(TOKEN_TILE)], axis=-1)


def _out_proj_body(x_ref, oa_ref, ob_ref, w_ref, nw_ref, wr_ref, br_ref, h_ref, hn_ref, e_ref, g_ref):
    h = (x_ref[...] + _bdot(oa_ref[...], w_ref[0:GDN_W, :]) + _bdot(ob_ref[...], w_ref[GDN_W:, :]))
    h_ref[...] = h
    hn = h * lax.rsqrt(jnp.mean(h * h, axis=-1, keepdims=True) + EPS) * nw_ref[...]
    _put_token_tiles(hn_ref, hn)
    hn_hi = hn.astype(BF16)
    hn_lo = (hn - hn_hi.astype(F32)).astype(BF16)
    logits = (jnp.dot(hn_hi, wr_ref[0], preferred_element_type=F32)
              + jnp.dot(hn_hi, wr_ref[1], preferred_element_type=F32)
              + jnp.dot(hn_lo, wr_ref[0], preferred_element_type=F32) + br_ref[...])
    lane = lax.broadcasted_iota(I32, logits.shape, 1)
    e_tile = jnp.zeros(logits.shape, I32)
    g_tile = jnp.zeros(logits.shape, F32)
    top0 = None
    den = None
    for k in range(TOP_K):
        mx = jnp.max(logits, axis=-1, keepdims=True)
        idx = jnp.min(jnp.where(logits == mx, lane, LANES), axis=-1, keepdims=True)
        logits = jnp.where(lane == idx, NEG_INF, logits)
        if k == 0:
            top0 = mx
        ex = jnp.exp(mx - top0)
        den = ex if k == 0 else den + ex
        e_tile = jnp.where(lane == k, idx, e_tile)
        g_tile = jnp.where(lane == k, ex, g_tile)
    e_ref[...] = e_tile
    g_ref[...] = g_tile / den


def _out_proj(x2, oa, ob, w_out, norm2_w, w_router, b_router, tm=512):
    t = x2.shape[0]
    wr = jnp.pad(w_router.astype(F32), ((0, 0), (0, LANES - N_EXPERTS)))
    wr_hi = wr.astype(BF16)
    wr = jnp.stack([wr_hi, (wr - wr_hi.astype(F32)).astype(BF16)])
    br = jnp.concatenate([b_router.astype(F32), jnp.full((LANES - N_EXPERTS,), NEG_INF, F32)]).reshape(1, LANES)
    row = lambda i: (i, 0)
    full = lambda i: (0, 0)
    return pl.pallas_call(
        _out_proj_body,
        grid=(t // tm,),
        in_specs=[
            pl.BlockSpec((tm, D_MODEL), row),
            pl.BlockSpec((tm, GDN_W), row),
            pl.BlockSpec((tm, SWA_W), row),
            pl.BlockSpec((GDN_W + SWA_W, D_MODEL), full),
            pl.BlockSpec((1, D_MODEL), full),
            pl.BlockSpec((2, D_MODEL, LANES), lambda i: (0, 0, 0)),
            pl.BlockSpec((1, LANES), full),
        ],
        out_specs=[
            pl.BlockSpec((tm, D_MODEL), row),
            pl.BlockSpec((tm * TOKEN_TILE, LANES), row),
            pl.BlockSpec((tm, LANES), row),
            pl.BlockSpec((tm, LANES), row),
        ],
        out_shape=[
            jax.ShapeDtypeStruct((t, D_MODEL), F32),
            jax.ShapeDtypeStruct((t * TOKEN_TILE, LANES), F32),
            jax.ShapeDtypeStruct((t, LANES), I32),
            jax.ShapeDtypeStruct((t, LANES), F32),
        ],
        compiler_params=_cparams(("parallel",)),
        name="out_proj",
    )(x2, oa, ob, w_out.astype(BF16), norm2_w.astype(F32).reshape(1, D_MODEL), wr, br)


def _moe_plan(top_e):
    t = top_e.shape[0]
    n = t * TOP_K
    p_rows = n + N_EXPERTS * MOE_BLOCK
    nblk = p_rows // MOE_BLOCK
    e_flat = top_e.reshape(-1)
    experts = jnp.arange(N_EXPERTS, dtype=I32)
    counts = jnp.sum((e_flat[:, None] == experts[None, :]).astype(I32), axis=0)
    padded = (counts + MOE_BLOCK - 1) // MOE_BLOCK * MOE_BLOCK
    pad_end = jnp.cumsum(padded)
    cand = jnp.arange(MOE_BLOCK, dtype=I32)[None, :] < (padded - counts)[:, None]
    cand_key = jnp.where(cand, 2 * experts[:, None] + 1, 2 * N_EXPERTS).reshape(-1)
    keys = jnp.concatenate([2 * e_flat, cand_key])
    id_bits = (p_rows - 1).bit_length()
    assert (2 * N_EXPERTS + 1) << id_bits < 2 ** 31
    packed = jnp.left_shift(keys, id_bits) | jnp.arange(p_rows, dtype=I32)
    src = jnp.bitwise_and(jnp.sort(packed), (1 << id_bits) - 1)
    is_pad = src >= n
    pad_rank = jnp.cumsum(is_pad.astype(I32)) - 1
    tok = jnp.right_shift(src, TOP_K.bit_length() - 1)
    gather_tok = jnp.where(is_pad, 0, tok)
    scatter_row = jnp.where(is_pad, n + pad_rank, jnp.bitwise_and(src, TOP_K - 1) * t + tok)
    blk_start = jnp.arange(nblk, dtype=I32) * MOE_BLOCK
    blk_expert = jnp.minimum(jnp.sum((blk_start[:, None] >= pad_end[None, :]).astype(I32), axis=1), N_EXPERTS - 1)
    return gather_tok.reshape(nblk, 1, MOE_BLOCK), scatter_row.reshape(nblk, 1, MOE_BLOCK), blk_expert


MOE_COL_CHUNK = 256


def _moe_body(bexp_ref, gidx_ref, sidx_ref, hn_ref, wgu_ref, bgu_ref, wdn_ref, bdn_ref, y_ref,
              xbuf, ybuf, wgu_bf, wdn_bf, gsem, ssem):
    i = pl.program_id(0)
    last = pl.num_programs(0) - 1
    slot = i % 2

    def gather_copy(tok, s, r):
        return pltpu.make_async_copy(hn_ref.at[tok], xbuf.at[s, pl.ds(r * TOKEN_TILE, TOKEN_TILE)], gsem.at[s])

    def scatter_copy(row, r):
        return pltpu.make_async_copy(ybuf.at[pl.ds(r * TOKEN_TILE, TOKEN_TILE)], y_ref.at[row], ssem)

    @pl.when(i == 0)
    def _():
        for r in range(MOE_BLOCK):
            gather_copy(gidx_ref[0, 0, r], 0, r).start()
        ybuf[...] = jnp.zeros_like(ybuf)

    @pl.when(i >= 0)
    def _():
        for r in range(MOE_BLOCK):
            scatter_copy(sidx_ref[0, 0, r], r).start(priority=r % 2)
            gather_copy(gidx_ref[1, 0, r], 1 - slot, r).start(priority=r % 2)

    @pl.when(jnp.logical_or(i == 0, bexp_ref[i] != bexp_ref[jnp.maximum(i - 1, 0)]))
    def _():
        wgu_bf[...] = wgu_ref[0].astype(BF16)
        wdn_bf[...] = wdn_ref[0].astype(BF16)

    for r in range(MOE_BLOCK):
        gather_copy(0, slot, r).wait()
    x = _get_token_tiles(xbuf.at[slot], MOE_BLOCK).astype(BF16)
    n_col = D_FF // MOE_COL_CHUNK
    acts = []
    for c in range(n_col):
        lo = c * MOE_COL_CHUNK
        g = jnp.dot(x, wgu_bf[:, lo:lo + MOE_COL_CHUNK], preferred_element_type=F32)
        u = jnp.dot(x, wgu_bf[:, D_FF + lo:D_FF + lo + MOE_COL_CHUNK], preferred_element_type=F32)
        g = jnp.minimum(g + bgu_ref[0, :, lo:lo + MOE_COL_CHUNK], SWIGLU_LIMIT)
        u = jnp.clip(u + bgu_ref[0, :, D_FF + lo:D_FF + lo + MOE_COL_CHUNK], -SWIGLU_LIMIT, SWIGLU_LIMIT)
        acts.append(((u + 1.0) * g * jax.nn.sigmoid(SWIGLU_ALPHA * g)).astype(BF16))
    act = jnp.concatenate(acts, axis=-1)
    outs = []
    for c in range(D_MODEL // MOE_COL_CHUNK):
        lo = c * MOE_COL_CHUNK
        outs.append(jnp.dot(act, wdn_bf[:, lo:lo + MOE_COL_CHUNK], preferred_element_type=F32)
                    + bdn_ref[0, :, lo:lo + MOE_COL_CHUNK])
    y = jnp.concatenate(outs, axis=-1)

    for r in range(MOE_BLOCK):
        scatter_copy(0, r).wait()
    _put_token_tiles(ybuf, y)

    @pl.when(i == last)
    def _():
        for r in range(MOE_BLOCK):
            scatter_copy(sidx_ref[1, 0, r], r).start()
        for r in range(MOE_BLOCK):
            scatter_copy(0, r).wait()
        for r in range(MOE_BLOCK):
            gather_copy(0, 1 - slot, r).wait()


def _moe_ffn(hn, plan, w_gu, b_gu, w_dn, b_dn):
    t = hn.shape[0] // TOKEN_TILE
    gather_tok, scatter_row, blk_expert = plan
    nblk = gather_tok.shape[0]
    n_rows = t * TOP_K + (N_EXPERTS + 1) * MOE_BLOCK
    gnext = jnp.concatenate([gather_tok[1:], gather_tok[-1:]], axis=0)
    gpair = jnp.concatenate([gather_tok, gnext], axis=1).reshape(nblk * 2, 1, MOE_BLOCK)
    before_first = (n_rows - MOE_BLOCK + jnp.arange(MOE_BLOCK, dtype=I32)).reshape(1, 1, MOE_BLOCK)
    sprev = jnp.concatenate([before_first, scatter_row[:-1]], axis=0)
    spair = jnp.concatenate([sprev, scatter_row], axis=1).reshape(nblk * 2, 1, MOE_BLOCK)
    wsel = lambda i, be: (be[i], 0, 0)
    pair = lambda i, be: (i, 0, 0)
    grid_spec = pltpu.PrefetchScalarGridSpec(
        num_scalar_prefetch=1,
        grid=(nblk,),
        in_specs=[
            pl.BlockSpec((2, 1, MOE_BLOCK), pair, memory_space=pltpu.SMEM),
            pl.BlockSpec((2, 1, MOE_BLOCK), pair, memory_space=pltpu.SMEM),
            pl.BlockSpec(memory_space=pl.ANY),
            pl.BlockSpec((1, D_MODEL, 2 * D_FF), wsel),
            pl.BlockSpec((1, 1, 2 * D_FF), wsel),
            pl.BlockSpec((1, D_FF, D_MODEL), wsel),
            pl.BlockSpec((1, 1, D_MODEL), wsel),
        ],
        out_specs=pl.BlockSpec(memory_space=pl.ANY),
        scratch_shapes=[
            pltpu.VMEM((2, MOE_BLOCK * TOKEN_TILE, LANES), F32),
            pltpu.VMEM((MOE_BLOCK * TOKEN_TILE, LANES), F32),
            pltpu.VMEM((D_MODEL, 2 * D_FF), BF16),
            pltpu.VMEM((D_FF, D_MODEL), BF16),
            pltpu.SemaphoreType.DMA((2,)),
            pltpu.SemaphoreType.DMA(()),
        ],
    )
    return pl.pallas_call(
        _moe_body,
        grid_spec=grid_spec,
        out_shape=jax.ShapeDtypeStruct((n_rows, TOKEN_TILE, LANES), F32),
        compiler_params=_cparams(("arbitrary",)),
        name="moe_ffn",
    )(blk_expert, gpair, spair, hn.reshape(t, TOKEN_TILE, LANES),
      w_gu.astype(F32), b_gu.astype(F32).reshape(N_EXPERTS, 1, 2 * D_FF),
      w_dn.astype(F32), b_dn.astype(F32).reshape(N_EXPERTS, 1, D_MODEL))


def _combine_body(h_ref, g_ref, *refs):
    y_refs, o_ref = refs[:TOP_K], refs[TOP_K]
    g = g_ref[...]
    m = g.shape[0]
    moe = g[:, 0:1] * _get_token_tiles(y_refs[0], m)
    for k in range(1, TOP_K):
        moe = moe + g[:, k:k + 1] * _get_token_tiles(y_refs[k], m)
    o_ref[...] = h_ref[...] + moe


def _combine(h, y_slots, gates, tm=512):
    t = h.shape[0]
    y2 = y_slots.reshape(-1, LANES)
    row = lambda i: (i, 0)
    kth = lambda k: (lambda i: (k * (t // tm) + i, 0))
    return pl.pallas_call(
        _combine_body,
        grid=(t // tm,),
        in_specs=[pl.BlockSpec((tm, D_MODEL), row), pl.BlockSpec((tm, LANES), row)]
        + [pl.BlockSpec((tm * TOKEN_TILE, LANES), kth(k)) for k in range(TOP_K)],
        out_specs=pl.BlockSpec((tm, D_MODEL), row),
        out_shape=jax.ShapeDtypeStruct((t, D_MODEL), F32),
        compiler_params=_cparams(("parallel",)),
        name="combine",
    )(h, gates, *([y2] * TOP_K))


def _layer(x, norm1_w, w_in, conv_w, a_log, dt_bias, gdn_norm_w, q_norm_w, k_norm_w, rel_bias,
           w_out, norm2_w, w_router, b_router, w_gate_up, b_gate_up, w_down, b_down):
    b, s, d = x.shape
    t = b * s
    x2 = x.reshape(t, d)
    qkv, gate, ba, qb, kb, vb = _in_proj(x2, norm1_w.astype(F32), _pack_w_in(w_in),
                                         q_norm_w.astype(F32), k_norm_w.astype(F32))
    oa = _gdn(qkv.reshape(b, s, CONV_DIM), gate.reshape(b, s, GDN_W), ba.reshape(b, s, LANES),
              conv_w, a_log, dt_bias, gdn_norm_w)
    ob = _dilated_attention(qb, kb, vb, rel_bias, b, s)
    h, hn, top_e, gates = _out_proj(x2, oa.reshape(t, GDN_W), ob, w_out, norm2_w, w_router, b_router)
    plan = _moe_plan(top_e[:, :TOP_K])
    y_slots = _moe_ffn(hn, plan, w_gate_up, b_gate_up, w_down, b_down)
    return _combine(h, y_slots, gates).reshape(b, s, d)


def kernel(x, norm1_w, w_in, conv_w, a_log, dt_bias, gdn_norm_w, q_norm_w, k_norm_w, rel_bias, w_out, norm2_w,
           w_router, b_router, w_gate_up, b_gate_up, w_down, b_down):
    h = x
    for l in range(norm1_w.shape[0]):
        h = _layer(h, norm1_w[l], w_in[l], conv_w[l], a_log[l], dt_bias[l], gdn_norm_w[l], q_norm_w[l],
                   k_norm_w[l], rel_bias, w_out[l], norm2_w[l], w_router[l], b_router[l], w_gate_up[l],
                   b_gate_up[l], w_down[l], b_down[l])
    return h
```

```python
import functools
import math

import numpy as np
import jax
import jax.numpy as jnp
from jax import lax
from jax.experimental import pallas as pl
from jax.experimental.pallas import tpu as pltpu

F32 = jnp.float32
BF16 = jnp.bfloat16
I32 = jnp.int32

EPS = 1e-6
D_MODEL = 1024
GDN_HEADS = 4
GDN_DK = 128
GDN_DV = 128
CONV_K = 4
SWA_HEADS = 8
SWA_DH = 64
DILATED_BRANCHES = ((128, 1), (512, 4), (2048, 16))
SWA_BLOCK = 128
REL_BUCKETS = 32
REL_MAX_DIST = 2048
N_EXPERTS = 32
TOP_K = 4
D_FF = D_MODEL
SWIGLU_LIMIT = 7.0
SWIGLU_ALPHA = 1.702
MOE_BLOCK = 256

LANES = 128
GDN_W = GDN_HEADS * GDN_DK
SWA_W = SWA_HEADS * SWA_DH
SWA_SLABS = SWA_W // LANES
CONV_DIM = 3 * GDN_W
COL_QKV = 0
COL_GATE = COL_QKV + CONV_DIM
COL_BA = COL_GATE + GDN_W
COL_QB = COL_BA + LANES
COL_KB = COL_QB + SWA_W
COL_VB = COL_KB + SWA_W
IN_PACKED = COL_VB + SWA_W

VMEM_LIMIT = 56 * 1024 * 1024

NEG_INF = float("-inf")


def _cparams(sem, **kw):
    return pltpu.CompilerParams(dimension_semantics=sem, vmem_limit_bytes=VMEM_LIMIT, **kw)


def _bdot(a, b):
    return jnp.dot(a.astype(BF16), b.astype(BF16), preferred_element_type=F32)


def _bdot_nt(a, b):
    return lax.dot_general(a.astype(BF16), b.astype(BF16), (((1,), (1,)), ((), ())),
                           preferred_element_type=F32)


def _bdot_tn(a, b):
    return lax.dot_general(a.astype(BF16), b.astype(BF16), (((0,), (0,)), ((), ())),
                           preferred_element_type=F32)


def _split3(x):
    hi = x.astype(BF16)
    r1 = x - hi.astype(F32)
    mid = r1.astype(BF16)
    lo = (r1 - mid.astype(F32)).astype(BF16)
    return hi, mid, lo


def _dot_split3(a_bf16, x):
    return sum(jnp.dot(a_bf16, part, preferred_element_type=F32) for part in _split3(x))


def _silu(x):
    return x * jax.nn.sigmoid(x)


def _in_proj_body(x_ref, nw_ref, w_ref, qnw_ref, knw_ref, hsum_ref,
                  qkv_ref, gate_ref, ba_ref, qb_ref, kb_ref, vb_ref):
    x = x_ref[...]
    ms = jnp.mean(x * x, axis=-1, keepdims=True)
    xn = (x * lax.rsqrt(ms + EPS) * nw_ref[...]).astype(BF16)

    def seg(lo, width):
        return jnp.dot(xn, w_ref[:, lo:lo + width], preferred_element_type=F32)

    qkv_ref[...] = seg(COL_QKV, CONV_DIM)
    gate_ref[...] = seg(COL_GATE, GDN_W)
    ba_ref[...] = seg(COL_BA, LANES)

    def head_rmsnorm(z, w_row, scale):
        sq = z * z
        hi = sq.astype(BF16)
        lo = (sq - hi.astype(F32)).astype(BF16)
        hmean = (jnp.dot(hi, hsum_ref[...], preferred_element_type=F32)
                 + jnp.dot(lo, hsum_ref[...], preferred_element_type=F32))
        return z * lax.rsqrt(hmean + EPS) * w_row * scale

    def put_slabs(ref, z):
        for j in range(SWA_SLABS):
            ref[j] = z[:, j * LANES:(j + 1) * LANES]

    put_slabs(qb_ref, head_rmsnorm(seg(COL_QB, SWA_W), qnw_ref[...], SWA_DH ** -0.5))
    put_slabs(kb_ref, head_rmsnorm(seg(COL_KB, SWA_W), knw_ref[...], 1.0))
    put_slabs(vb_ref, seg(COL_VB, SWA_W))


def _in_proj(x2, norm1_w, w_packed, q_norm_w, k_norm_w, tm=512):
    t = x2.shape[0]
    hsum = jnp.asarray(np.kron(np.eye(SWA_HEADS), np.ones((SWA_DH, SWA_DH))) / SWA_DH, BF16)
    row = lambda i: (i, 0)
    full = lambda i: (0, 0)
    slab = lambda i: (0, i, 0)
    return pl.pallas_call(
        _in_proj_body,
        grid=(t // tm,),
        in_specs=[
            pl.BlockSpec((tm, D_MODEL), row),
            pl.BlockSpec((1, D_MODEL), full),
            pl.BlockSpec((D_MODEL, IN_PACKED), full),
            pl.BlockSpec((1, SWA_W), full),
            pl.BlockSpec((1, SWA_W), full),
            pl.BlockSpec((SWA_W, SWA_W), full),
        ],
        out_specs=[
            pl.BlockSpec((tm, CONV_DIM), row),
            pl.BlockSpec((tm, GDN_W), row),
            pl.BlockSpec((tm, LANES), row),
            pl.BlockSpec((SWA_SLABS, tm, LANES), slab),
            pl.BlockSpec((SWA_SLABS, tm, LANES), slab),
            pl.BlockSpec((SWA_SLABS, tm, LANES), slab),
        ],
        out_shape=[
            jax.ShapeDtypeStruct((t, CONV_DIM), F32),
            jax.ShapeDtypeStruct((t, GDN_W), F32),
            jax.ShapeDtypeStruct((t, LANES), F32),
            jax.ShapeDtypeStruct((SWA_SLABS, t, LANES), F32),
            jax.ShapeDtypeStruct((SWA_SLABS, t, LANES), F32),
            jax.ShapeDtypeStruct((SWA_SLABS, t, LANES), F32),
        ],
        compiler_params=_cparams(("parallel",)),
        name="in_proj",
    )(x2, norm1_w.reshape(1, D_MODEL), w_packed,
      jnp.tile(q_norm_w, SWA_HEADS).reshape(1, SWA_W), jnp.tile(k_norm_w, SWA_HEADS).reshape(1, SWA_W), hsum)


def _pack_w_in(w_in):
    o = np.cumsum([0, GDN_W, GDN_W, GDN_W, GDN_W, GDN_HEADS, GDN_HEADS, SWA_W, SWA_W, SWA_W])
    ba = jnp.pad(w_in[:, o[4]:o[6]], ((0, 0), (0, LANES - 2 * GDN_HEADS)))
    return jnp.concatenate([w_in[:, o[0]:o[4]], ba, w_in[:, o[6]:o[9]]], axis=1).astype(BF16)


def _unit_lower_inverses(a_list, n, blk):
    ii = lax.broadcasted_iota(I32, (n, n), 0)
    jj = lax.broadcasted_iota(I32, (n, n), 1)
    same = lambda sh: jnp.right_shift(ii, sh) == jnp.right_shift(jj, sh)
    eye = (ii == jj).astype(F32)
    pair = same(1)
    t_list = [eye - jnp.where(pair, a, 0.0) for a in a_list]
    sh = 1
    while (1 << sh) < blk:
        join = same(sh + 1) & jnp.logical_not(same(sh))
        te = [_bdot(t, jnp.where(join, a, 0.0)) for t, a in zip(t_list, a_list)]
        tet = [_bdot(x, t) for x, t in zip(te, t_list)]
        t_list = [t - y for t, y in zip(t_list, tet)]
        sh += 1
    return t_list


def _gdn_body(chunk, n_chunks, qkv_ref, gate_ref, ba_ref, convw_ref, alane_ref, dtlane_ref, normw_ref,
              o_ref, tail_ref, state_ref):
    c = chunk
    rows = chunk * n_chunks

    @pl.when(pl.program_id(1) == 0)
    def _():
        tail_ref[...] = jnp.zeros_like(tail_ref)
        state_ref[...] = jnp.zeros_like(state_ref)

    x = qkv_ref[0]
    xs = jnp.concatenate([tail_ref[...], x], axis=0)
    w = convw_ref[...]
    conv = (xs[5:5 + rows] * w[0:1] + xs[6:6 + rows] * w[1:2] + xs[7:7 + rows] * w[2:3] + x * w[3:4])
    tail_ref[...] = x[rows - 8:rows]
    y = _silu(conv)

    ba = ba_ref[0]
    beta_all = jax.nn.sigmoid(ba)
    z = ba + dtlane_ref[...]
    softplus = jnp.maximum(z, 0.0) + jnp.log(1.0 + jnp.exp(-jnp.abs(z)))
    g_all = -alane_ref[...] * softplus
    ri = lax.broadcasted_iota(I32, (rows, rows), 0)
    ci = lax.broadcasted_iota(I32, (rows, rows), 1)
    csh = c.bit_length() - 1
    tri = ((ri >= ci) & (jnp.right_shift(ri, csh) == jnp.right_shift(ci, csh))).astype(F32)
    gc_all = _dot_split3(tri.astype(BF16), g_all)

    heads = range(GDN_HEADS)
    qn, kn, vv = [], [], []
    for h in heads:
        lo = h * GDN_DK
        q = y[:, lo:lo + GDN_DK]
        k = y[:, GDN_W + lo:GDN_W + lo + GDN_DK]
        qn.append(q * lax.rsqrt(jnp.sum(q * q, axis=-1, keepdims=True) + EPS) * (GDN_DK ** -0.5))
        kn.append(k * lax.rsqrt(jnp.sum(k * k, axis=-1, keepdims=True) + EPS))
        vv.append(y[:, 2 * GDN_W + lo:2 * GDN_W + lo + GDN_DV])

    hs = GDN_HEADS * c
    hsh = csh
    ii = lax.broadcasted_iota(I32, (hs, hs), 0)
    jj = lax.broadcasted_iota(I32, (hs, hs), 1)
    same_head = jnp.right_shift(ii, hsh) == jnp.right_shift(jj, hsh)
    row_head = jnp.right_shift(lax.broadcasted_iota(I32, (hs, GDN_DV), 0), hsh)

    def stack(parts, r0):
        return jnp.concatenate([p[r0:r0 + c] for p in parts], axis=0)

    def col(src, lane0, r0):
        return jnp.concatenate([src[r0:r0 + c, lane0 + h:lane0 + h + 1] for h in heads], axis=0)

    def diag_blocks(p):
        out = jnp.where(row_head == 0, p[:, 0:GDN_DV], 0.0)
        for h in range(1, GDN_HEADS):
            out = out + jnp.where(row_head == h, p[:, h * GDN_DV:(h + 1) * GDN_DV], 0.0)
        return out

    q_st, u_rhs, kdec, cdecay, a_kk, a_qk = ([] for _ in range(6))
    for n in range(n_chunks):
        r0 = n * c
        k = stack(kn, r0)
        q = stack(qn, r0)
        v = stack(vv, r0)
        beta = col(beta_all, 0, r0)
        gcol = col(gc_all, GDN_HEADS, r0)
        grow = jnp.transpose(jnp.broadcast_to(gcol, (hs, LANES)))[0:1, :]
        glast = jnp.concatenate(
            [jnp.broadcast_to(gc_all[r0 + c - 1:r0 + c, GDN_HEADS + h:GDN_HEADS + h + 1], (c, 1)) for h in heads],
            axis=0)
        dec = jnp.exp(jnp.where(same_head & (ii >= jj), gcol - grow, NEG_INF))
        eg = jnp.exp(gcol)
        kb = k * beta
        prod = _bdot_nt(jnp.concatenate([kb, q], axis=0), k)
        a_kk.append(jnp.where(ii > jj, prod[:hs] * dec, 0.0))
        a_qk.append(prod[hs:] * dec)
        q_st.append(q * eg)
        u_rhs.append(jnp.concatenate([v * beta, kb * eg], axis=-1))
        kdec.append(k * jnp.exp(glast - gcol))
        cdecay.append(jnp.concatenate(
            [jnp.broadcast_to(jnp.exp(gc_all[r0 + c - 1:r0 + c, GDN_HEADS + h:GDN_HEADS + h + 1]), (1, GDN_DV))
             for h in heads], axis=-1))
    t_inv = _unit_lower_inverses(a_kk, hs, c)
    sols = [_bdot(t, r) for t, r in zip(t_inv, u_rhs)]

    gate = gate_ref[0]
    normw = normw_ref[...]
    s_cat = state_ref[...]
    for n in range(n_chunks):
        r0 = n * c
        u, wk = sols[n][:, :GDN_DV], sols[n][:, GDN_DV:]
        p = _bdot(jnp.concatenate([wk, q_st[n]], axis=0), s_cat)
        v_new = u - diag_blocks(p[:hs])
        o = diag_blocks(p[hs:]) + _bdot(a_qk[n], v_new)
        v_bd = jnp.concatenate([jnp.where(row_head == h, v_new, 0.0) for h in heads], axis=-1)
        s_cat = s_cat * cdecay[n] + _bdot_tn(kdec[n], v_bd)
        o = o * lax.rsqrt(jnp.mean(o * o, axis=-1, keepdims=True) + EPS) * normw
        for h in heads:
            lo = h * GDN_DV
            o_ref[0, r0:r0 + c, lo:lo + GDN_DV] = o[h * c:(h + 1) * c] * _silu(gate[r0:r0 + c, lo:lo + GDN_DV])
    state_ref[...] = s_cat


def _gdn(qkv, gate, ba, conv_w, a_log, dt_bias, gdn_norm_w, chunk=64, n_chunks=8):
    b, s, _ = qkv.shape
    rows = chunk * n_chunks
    pad = jnp.zeros((LANES - 2 * GDN_HEADS,), F32)
    alane = jnp.concatenate([jnp.zeros((GDN_HEADS,), F32), jnp.exp(a_log.astype(F32)), pad]).reshape(1, LANES)
    dtlane = jnp.concatenate([jnp.zeros((GDN_HEADS,), F32), dt_bias.astype(F32), pad]).reshape(1, LANES)
    blk = lambda i, j: (i, j, 0)
    full = lambda i, j: (0, 0)
    return pl.pallas_call(
        functools.partial(_gdn_body, chunk, n_chunks),
        grid=(b, s // rows),
        in_specs=[
            pl.BlockSpec((1, rows, CONV_DIM), blk),
            pl.BlockSpec((1, rows, GDN_W), blk),
            pl.BlockSpec((1, rows, LANES), blk),
            pl.BlockSpec((CONV_K, CONV_DIM), full),
            pl.BlockSpec((1, LANES), full),
            pl.BlockSpec((1, LANES), full),
            pl.BlockSpec((1, GDN_DV), full),
        ],
        out_specs=pl.BlockSpec((1, rows, GDN_W), blk),
        out_shape=jax.ShapeDtypeStruct((b, s, GDN_W), F32),
        scratch_shapes=[pltpu.VMEM((8, CONV_DIM), F32), pltpu.VMEM((GDN_DK, GDN_HEADS * GDN_DV), F32)],
        compiler_params=_cparams(("parallel", "arbitrary")),
        name="gdn",
    )(qkv, gate, ba, conv_w.astype(F32), alane, dtlane, gdn_norm_w.astype(F32).reshape(1, GDN_DV))


def _t5_bucket(dist):
    max_exact = REL_BUCKETS // 2
    n = np.maximum(dist, 0)
    large = max_exact + (np.log(np.maximum(n, 1) / max_exact) / math.log(REL_MAX_DIST / max_exact)
                         * (REL_BUCKETS - max_exact)).astype(np.int32)
    large = np.minimum(large, REL_BUCKETS - 1)
    return np.where(n < max_exact, n, large).astype(np.int32)


def _branch_bias(rel_bias, window, dilation):
    kj = np.arange(2 * SWA_BLOCK)[:, None]
    qi = np.arange(SWA_BLOCK)[None, :]
    steps = qi + SWA_BLOCK - kj
    valid = (steps >= 0) & (steps <= window // dilation)
    onehot = np.eye(REL_BUCKETS, dtype=np.float32)[_t5_bucket(steps * dilation).reshape(-1)]
    bias = jnp.dot(jnp.asarray(onehot), rel_bias.astype(F32), precision=lax.Precision.HIGHEST)
    bias = bias.reshape(2 * SWA_BLOCK, SWA_BLOCK, SWA_HEADS).transpose(2, 0, 1)
    bias = jnp.where(jnp.asarray(valid)[None], bias, NEG_INF)
    return jnp.concatenate([bias[:, :SWA_BLOCK], jnp.full((SWA_HEADS, SWA_BLOCK, SWA_BLOCK), NEG_INF, F32),
                            bias[:, SWA_BLOCK:]], axis=1)


SWA_SUPER = 2048
STAT_ROWS = 16


def _swa_body(q_ref, k_ref, v_ref, bias_ref, spread_ref, o_ref, acco, accl):
    sb = pl.program_id(1)
    n_qblk = SWA_SUPER // SWA_BLOCK
    stat_pad = jnp.zeros((LANES - SWA_HEADS, SWA_BLOCK), F32)
    last_branch = len(DILATED_BRANCHES) - 1

    for bi, (_, d) in enumerate(DILATED_BRANCHES):
        dsh = d.bit_length() - 1
        per_class = n_qblk // d

        def rows(start, d=d):
            if d > 1:
                return pl.ds(start, SWA_BLOCK, stride=d)
            return pl.ds(pl.multiple_of(start, SWA_BLOCK), SWA_BLOCK)

        def qblock(idx, carry, bi=bi, d=d, dsh=dsh, per_class=per_class, rows=rows):
            r = jnp.bitwise_and(idx, d - 1)
            n = jnp.right_shift(idx, dsh)
            loc = r + n * (d * SWA_BLOCK)
            cur = sb * SWA_SUPER + loc
            first = (sb * per_class + n) == 0
            prev = jnp.maximum(cur - d * SWA_BLOCK, r)
            poff = pl.multiple_of(jnp.where(first, SWA_BLOCK, 0), SWA_BLOCK)
            scores, values = [], []
            for j in range(SWA_SLABS):
                qb = q_ref[j, rows(loc), :].astype(BF16)
                kw = jnp.concatenate([k_ref[j, rows(prev), :], k_ref[j, rows(cur), :]], axis=0).astype(BF16)
                vw = jnp.concatenate([v_ref[j, rows(prev), :], v_ref[j, rows(cur), :]], axis=0).astype(BF16)
                for lo in range(0, LANES, SWA_DH):
                    scores.append(_bdot_nt(kw[:, lo:lo + SWA_DH], qb[:, lo:lo + SWA_DH]))
                    values.append(vw[:, lo:lo + SWA_DH])
            mxs, dens, probs = [], [], []
            for h in range(SWA_HEADS):
                bias = jnp.concatenate([bias_ref[bi, h, pl.ds(poff, SWA_BLOCK), :],
                                        bias_ref[bi, h, 2 * SWA_BLOCK:3 * SWA_BLOCK, :]], axis=0)
                st = scores[h] + bias
                mx = jnp.max(st, axis=0, keepdims=True)
                p = jnp.exp(st - mx)
                mxs.append(mx)
                dens.append(jnp.sum(p, axis=0, keepdims=True))
                probs.append(p.astype(BF16))
            outs = [_bdot_tn(values[h], probs[h]) for h in range(SWA_HEADS)]
            mx = jnp.concatenate(mxs, axis=0)
            den = jnp.concatenate(dens, axis=0)
            lse = mx + jnp.log(den)
            if bi == 0:
                scale = 1.0 / den
                new = lse
            else:
                lacc = jnp.transpose(accl[rows(loc), :])[0:SWA_HEADS]
                m2 = jnp.maximum(lacc, lse)
                ea = jnp.exp(lacc - m2)
                tot = ea + jnp.exp(lse - m2)
                scale = jnp.exp(mx - m2) / tot
                keep = ea / tot
                new = m2 + jnp.log(tot)
            o_tok = jnp.transpose(
                jnp.concatenate([outs[h] * scale[h:h + 1] for h in range(SWA_HEADS)], axis=0))
            if bi > 0:
                kpad = jnp.concatenate([keep, jnp.zeros((STAT_ROWS - SWA_HEADS, SWA_BLOCK), F32)], axis=0)
                hi = kpad.astype(BF16)
                lo2 = (kpad - hi.astype(F32)).astype(BF16)
                spread = spread_ref[...]
                keep_tok = _bdot_tn(hi, spread) + _bdot_tn(lo2, spread)
            for j in range(SWA_SLABS):
                o_slab = o_tok[:, j * LANES:(j + 1) * LANES]
                if bi > 0:
                    o_slab = acco[j, rows(loc), :] * keep_tok[:, j * LANES:(j + 1) * LANES] + o_slab
                acco[j, rows(loc), :] = o_slab
            if bi < last_branch:
                accl[rows(loc), :] = jnp.transpose(jnp.concatenate([new, stat_pad], axis=0))
            return carry

        lax.fori_loop(0, n_qblk, qblock, 0)
    for j in range(SWA_SLABS):
        o_ref[:, j * LANES:(j + 1) * LANES] = acco[j].astype(BF16)


def _dilated_attention(qb, kb, vb, rel_bias, b, s):
    bias = jnp.stack([_branch_bias(rel_bias, w, d) for w, d in DILATED_BRANCHES])
    spread = np.zeros((STAT_ROWS, SWA_W), np.float32)
    for h in range(SWA_HEADS):
        spread[h, h * SWA_DH:(h + 1) * SWA_DH] = 1.0
    once = pl.Buffered(1)
    spans = s // SWA_SUPER
    span = lambda i, j: (0, i * spans + j, 0)
    seq = lambda i, j: (0, i, 0)
    return pl.pallas_call(
        _swa_body,
        grid=(b, spans),
        in_specs=[
            pl.BlockSpec((SWA_SLABS, SWA_SUPER, LANES), span),
            pl.BlockSpec((SWA_SLABS, s, LANES), seq, pipeline_mode=once),
            pl.BlockSpec((SWA_SLABS, s, LANES), seq, pipeline_mode=once),
            pl.BlockSpec(bias.shape, lambda i, j: (0, 0, 0, 0), pipeline_mode=once),
            pl.BlockSpec((STAT_ROWS, SWA_W), lambda i, j: (0, 0)),
        ],
        out_specs=pl.BlockSpec((SWA_SUPER, SWA_W), lambda i, j: (i * spans + j, 0)),
        out_shape=jax.ShapeDtypeStruct((b * s, SWA_W), BF16),
        scratch_shapes=[pltpu.VMEM((SWA_SLABS, SWA_SUPER, LANES), F32), pltpu.VMEM((SWA_SUPER, LANES), F32)],
        compiler_params=_cparams(("parallel", "arbitrary")),
        name="swa",
    )(qb, kb, vb, bias, jnp.asarray(spread, BF16))


TOKEN_TILE = D_MODEL // LANES


def _put_token_tiles(ref, rows):
    m = rows.shape[0]
    for s in range(TOKEN_TILE):
        ref[pl.ds(s, m, stride=TOKEN_TILE), :] = rows[:, s * LANES:(s + 1) * LANES]


def _get_token_tiles(ref, m, base=0):
    return jnp.concatenate([ref[pl.ds(base + s, m, stride=TOKEN_TILE), :] for s in range(TOKEN_TILE)], axis=-1)


def _out_proj_body(x_ref, oa_ref, ob_ref, w_ref, nw_ref, wr_ref, br_ref, h_ref, hn_ref, e_ref, g_ref):
    h = (x_ref[...] + _bdot(oa_ref[...], w_ref[0:GDN_W, :]) + _bdot(ob_ref[...], w_ref[GDN_W:, :]))
    h_ref[...] = h
    hn = h * lax.rsqrt(jnp.mean(h * h, axis=-1, keepdims=True) + EPS) * nw_ref[...]
    _put_token_tiles(hn_ref, hn)
    hn_hi = hn.astype(BF16)
    hn_lo = (hn - hn_hi.astype(F32)).astype(BF16)
    logits = (jnp.dot(hn_hi, wr_ref[0], preferred_element_type=F32)
              + jnp.dot(hn_hi, wr_ref[1], preferred_element_type=F32)
              + jnp.dot(hn_lo, wr_ref[0], preferred_element_type=F32) + br_ref[...])
    lane = lax.broadcasted_iota(I32, logits.shape, 1)
    e_tile = jnp.zeros(logits.shape, I32)
    g_tile = jnp.zeros(logits.shape, F32)
    top0 = None
    den = None
    for k in range(TOP_K):
        mx = jnp.max(logits, axis=-1, keepdims=True)
        idx = jnp.min(jnp.where(logits == mx, lane, LANES), axis=-1, keepdims=True)
        logits = jnp.where(lane == idx, NEG_INF, logits)
        if k == 0:
            top0 = mx
        ex = jnp.exp(mx - top0)
        den = ex if k == 0 else den + ex
        e_tile = jnp.where(lane == k, idx, e_tile)
        g_tile = jnp.where(lane == k, ex, g_tile)
    e_ref[...] = e_tile
    g_ref[...] = g_tile / den


def _out_proj(x2, oa, ob, w_out, norm2_w, w_router, b_router, tm=512):
    t = x2.shape[0]
    wr = jnp.pad(w_router.astype(F32), ((0, 0), (0, LANES - N_EXPERTS)))
    wr_hi = wr.astype(BF16)
    wr = jnp.stack([wr_hi, (wr - wr_hi.astype(F32)).astype(BF16)])
    br = jnp.concatenate([b_router.astype(F32), jnp.full((LANES - N_EXPERTS,), NEG_INF, F32)]).reshape(1, LANES)
    row = lambda i: (i, 0)
    full = lambda i: (0, 0)
    return pl.pallas_call(
        _out_proj_body,
        grid=(t // tm,),
        in_specs=[
            pl.BlockSpec((tm, D_MODEL), row),
            pl.BlockSpec((tm, GDN_W), row),
            pl.BlockSpec((tm, SWA_W), row),
            pl.BlockSpec((GDN_W + SWA_W, D_MODEL), full),
            pl.BlockSpec((1, D_MODEL), full),
            pl.BlockSpec((2, D_MODEL, LANES), lambda i: (0, 0, 0)),
            pl.BlockSpec((1, LANES), full),
        ],
        out_specs=[
            pl.BlockSpec((tm, D_MODEL), row),
            pl.BlockSpec((tm * TOKEN_TILE, LANES), row),
            pl.BlockSpec((tm, LANES), row),
            pl.BlockSpec((tm, LANES), row),
        ],
        out_shape=[
            jax.ShapeDtypeStruct((t, D_MODEL), F32),
            jax.ShapeDtypeStruct((t * TOKEN_TILE, LANES), F32),
            jax.ShapeDtypeStruct((t, LANES), I32),
            jax.ShapeDtypeStruct((t, LANES), F32),
        ],
        compiler_params=_cparams(("parallel",)),
        name="out_proj",
    )(x2, oa, ob, w_out.astype(BF16), norm2_w.astype(F32).reshape(1, D_MODEL), wr, br)


def _moe_plan(top_e):
    t = top_e.shape[0]
    n = t * TOP_K
    p_rows = n + N_EXPERTS * MOE_BLOCK
    nblk = p_rows // MOE_BLOCK
    e_flat = top_e.reshape(-1)
    experts = jnp.arange(N_EXPERTS, dtype=I32)
    counts = jnp.sum((e_flat[:, None] == experts[None, :]).astype(I32), axis=0)
    padded = (counts + MOE_BLOCK - 1) // MOE_BLOCK * MOE_BLOCK
    pad_end = jnp.cumsum(padded)
    cand = jnp.arange(MOE_BLOCK, dtype=I32)[None, :] < (padded - counts)[:, None]
    cand_key = jnp.where(cand, 2 * experts[:, None] + 1, 2 * N_EXPERTS).reshape(-1)
    keys = jnp.concatenate([2 * e_flat, cand_key])
    id_bits = (p_rows - 1).bit_length()
    assert (2 * N_EXPERTS + 1) << id_bits < 2 ** 31
    packed = jnp.left_shift(keys, id_bits) | jnp.arange(p_rows, dtype=I32)
    src = jnp.bitwise_and(jnp.sort(packed), (1 << id_bits) - 1)
    is_pad = src >= n
    pad_rank = jnp.cumsum(is_pad.astype(I32)) - 1
    tok = jnp.right_shift(src, TOP_K.bit_length() - 1)
    gather_tok = jnp.where(is_pad, 0, tok)
    scatter_row = jnp.where(is_pad, n + pad_rank, jnp.bitwise_and(src, TOP_K - 1) * t + tok)
    blk_start = jnp.arange(nblk, dtype=I32) * MOE_BLOCK
    blk_expert = jnp.minimum(jnp.sum((blk_start[:, None] >= pad_end[None, :]).astype(I32), axis=1), N_EXPERTS - 1)
    return gather_tok.reshape(nblk, 1, MOE_BLOCK), scatter_row.reshape(nblk, 1, MOE_BLOCK), blk_expert


MOE_COL_CHUNK = 256


def _moe_body(bexp_ref, gidx_ref, sidx_ref, hn_ref, wgu_ref, bgu_ref, wdn_ref, bdn_ref, y_ref,
              xbuf, ybuf, wgu_bf, wdn_bf, gsem, ssem):
    i = pl.program_id(0)
    last = pl.num_programs(0) - 1
    slot = i % 2

    def gather_copy(tok, s, r):
        return pltpu.make_async_copy(hn_ref.at[tok], xbuf.at[s, pl.ds(r * TOKEN_TILE, TOKEN_TILE)], gsem.at[s])

    def scatter_copy(row, r):
        return pltpu.make_async_copy(ybuf.at[pl.ds(r * TOKEN_TILE, TOKEN_TILE)], y_ref.at[row], ssem)

    @pl.when(i == 0)
    def _():
        for r in range(MOE_BLOCK):
            gather_copy(gidx_ref[0, 0, r], 0, r).start()
        ybuf[...] = jnp.zeros_like(ybuf)

    @pl.when(i >= 0)
    def _():
        for r in range(MOE_BLOCK):
            scatter_copy(sidx_ref[0, 0, r], r).start(priority=r % 2)
            gather_copy(gidx_ref[1, 0, r], 1 - slot, r).start(priority=r % 2)

    @pl.when(jnp.logical_or(i == 0, bexp_ref[i] != bexp_ref[jnp.maximum(i - 1, 0)]))
    def _():
        wgu_bf[...] = wgu_ref[0].astype(BF16)
        wdn_bf[...] = wdn_ref[0].astype(BF16)

    for r in range(MOE_BLOCK):
        gather_copy(0, slot, r).wait()
    x = _get_token_tiles(xbuf.at[slot], MOE_BLOCK).astype(BF16)
    n_col = D_FF // MOE_COL_CHUNK
    acts = []
    for c in range(n_col):
        lo = c * MOE_COL_CHUNK
        g = jnp.dot(x, wgu_bf[:, lo:lo + MOE_COL_CHUNK], preferred_element_type=F32)
        u = jnp.dot(x, wgu_bf[:, D_FF + lo:D_FF + lo + MOE_COL_CHUNK], preferred_element_type=F32)
        g = jnp.minimum(g + bgu_ref[0, :, lo:lo + MOE_COL_CHUNK], SWIGLU_LIMIT)
        u = jnp.clip(u + bgu_ref[0, :, D_FF + lo:D_FF + lo + MOE_COL_CHUNK], -SWIGLU_LIMIT, SWIGLU_LIMIT)
        acts.append(((u + 1.0) * g * jax.nn.sigmoid(SWIGLU_ALPHA * g)).astype(BF16))
    act = jnp.concatenate(acts, axis=-1)
    outs = []
    for c in range(D_MODEL // MOE_COL_CHUNK):
        lo = c * MOE_COL_CHUNK
        outs.append(jnp.dot(act, wdn_bf[:, lo:lo + MOE_COL_CHUNK], preferred_element_type=F32)
                    + bdn_ref[0, :, lo:lo + MOE_COL_CHUNK])
    y = jnp.concatenate(outs, axis=-1)

    for r in range(MOE_BLOCK):
        scatter_copy(0, r).wait()
    _put_token_tiles(ybuf, y)

    @pl.when(i == last)
    def _():
        for r in range(MOE_BLOCK):
            scatter_copy(sidx_ref[1, 0, r], r).start()
        for r in range(MOE_BLOCK):
            scatter_copy(0, r).wait()
        for r in range(MOE_BLOCK):
            gather_copy(0, 1 - slot, r).wait()


def _moe_ffn(hn, plan, w_gu, b_gu, w_dn, b_dn):
    t = hn.shape[0] // TOKEN_TILE
    gather_tok, scatter_row, blk_expert = plan
    nblk = gather_tok.shape[0]
    n_rows = t * TOP_K + (N_EXPERTS + 1) * MOE_BLOCK
    gnext = jnp.concatenate([gather_tok[1:], gather_tok[-1:]], axis=0)
    gpair = jnp.concatenate([gather_tok, gnext], axis=1).reshape(nblk * 2, 1, MOE_BLOCK)
    before_first = (n_rows - MOE_BLOCK + jnp.arange(MOE_BLOCK, dtype=I32)).reshape(1, 1, MOE_BLOCK)
    sprev = jnp.concatenate([before_first, scatter_row[:-1]], axis=0)
    spair = jnp.concatenate([sprev, scatter_row], axis=1).reshape(nblk * 2, 1, MOE_BLOCK)
    wsel = lambda i, be: (be[i], 0, 0)
    pair = lambda i, be: (i, 0, 0)
    grid_spec = pltpu.PrefetchScalarGridSpec(
        num_scalar_prefetch=1,
        grid=(nblk,),
        in_specs=[
            pl.BlockSpec((2, 1, MOE_BLOCK), pair, memory_space=pltpu.SMEM),
            pl.BlockSpec((2, 1, MOE_BLOCK), pair, memory_space=pltpu.SMEM),
            pl.BlockSpec(memory_space=pl.ANY),
            pl.BlockSpec((1, D_MODEL, 2 * D_FF), wsel),
            pl.BlockSpec((1, 1, 2 * D_FF), wsel),
            pl.BlockSpec((1, D_FF, D_MODEL), wsel),
            pl.BlockSpec((1, 1, D_MODEL), wsel),
        ],
        out_specs=pl.BlockSpec(memory_space=pl.ANY),
        scratch_shapes=[
            pltpu.VMEM((2, MOE_BLOCK * TOKEN_TILE, LANES), F32),
            pltpu.VMEM((MOE_BLOCK * TOKEN_TILE, LANES), F32),
            pltpu.VMEM((D_MODEL, 2 * D_FF), BF16),
            pltpu.VMEM((D_FF, D_MODEL), BF16),
            pltpu.SemaphoreType.DMA((2,)),
            pltpu.SemaphoreType.DMA(()),
        ],
    )
    return pl.pallas_call(
        _moe_body,
        grid_spec=grid_spec,
        out_shape=jax.ShapeDtypeStruct((n_rows, TOKEN_TILE, LANES), F32),
        compiler_params=_cparams(("arbitrary",)),
        name="moe_ffn",
    )(blk_expert, gpair, spair, hn.reshape(t, TOKEN_TILE, LANES),
      w_gu.astype(F32), b_gu.astype(F32).reshape(N_EXPERTS, 1, 2 * D_FF),
      w_dn.astype(F32), b_dn.astype(F32).reshape(N_EXPERTS, 1, D_MODEL))


def _combine_body(h_ref, g_ref, *refs):
    y_refs, o_ref = refs[:TOP_K], refs[TOP_K]
    g = g_ref[...]
    m = g.shape[0]
    moe = g[:, 0:1] * _get_token_tiles(y_refs[0], m)
    for k in range(1, TOP_K):
        moe = moe + g[:, k:k + 1] * _get_token_tiles(y_refs[k], m)
    o_ref[...] = h_ref[...] + moe


def _combine(h, y_slots, gates, tm=512):
    t = h.shape[0]
    y2 = y_slots.reshape(-1, LANES)
    row = lambda i: (i, 0)
    kth = lambda k: (lambda i: (k * (t // tm) + i, 0))
    return pl.pallas_call(
        _combine_body,
        grid=(t // tm,),
        in_specs=[pl.BlockSpec((tm, D_MODEL), row), pl.BlockSpec((tm, LANES), row)]
        + [pl.BlockSpec((tm * TOKEN_TILE, LANES), kth(k)) for k in range(TOP_K)],
        out_specs=pl.BlockSpec((tm, D_MODEL), row),
        out_shape=jax.ShapeDtypeStruct((t, D_MODEL), F32),
        compiler_params=_cparams(("parallel",)),
        name="combine",
    )(h, gates, *([y2] * TOP_K))


def _layer(x, norm1_w, w_in, conv_w, a_log, dt_bias, gdn_norm_w, q_norm_w, k_norm_w, rel_bias,
           w_out, norm2_w, w_router, b_router, w_gate_up, b_gate_up, w_down, b_down):
    b, s, d = x.shape
    t = b * s
    x2 = x.reshape(t, d)
    qkv, gate, ba, qb, kb, vb = _in_proj(x2, norm1_w.astype(F32), _pack_w_in(w_in),
                                         q_norm_w.astype(F32), k_norm_w.astype(F32))
    oa = _gdn(qkv.reshape(b, s, CONV_DIM), gate.reshape(b, s, GDN_W), ba.reshape(b, s, LANES),
              conv_w, a_log, dt_bias, gdn_norm_w)
    ob = _dilated_attention(qb, kb, vb, rel_bias, b, s)
    h, hn, top_e, gates = _out_proj(x2, oa.reshape(t, GDN_W), ob, w_out, norm2_w, w_router, b_router)
    plan = _moe_plan(top_e[:, :TOP_K])
    y_slots = _moe_ffn(hn, plan, w_gate_up, b_gate_up, w_down, b_down)
    return _combine(h, y_slots, gates).reshape(b, s, d)


def kernel(x, norm1_w, w_in, conv_w, a_log, dt_bias, gdn_norm_w, q_norm_w, k_norm_w, rel_bias, w_out, norm2_w,
           w_router, b_router, w_gate_up, b_gate_up, w_down, b_down):
    h = x
    for l in range(norm1_w.shape[0]):
        h = _layer(h, norm1_w[l], w_in[l], conv_w[l], a_log[l], dt_bias[l], gdn_norm_w[l], q_norm_w[l],
                   k_norm_w[l], rel_bias, w_out[l], norm2_w[l], w_router[l], b_router[l], w_gate_up[l],
                   b_gate_up[l], w_down[l], b_down[l])
    return h
```

```python
import functools
import math

import numpy as np
import jax
import jax.numpy as jnp
from jax import lax
from jax.experimental import pallas as pl
from jax.experimental.pallas import tpu as pltpu

F32 = jnp.float32
BF16 = jnp.bfloat16
I32 = jnp.int32

EPS = 1e-6
D_MODEL = 1024
GDN_HEADS = 4
GDN_DK = 128
GDN_DV = 128
CONV_K = 4
SWA_HEADS = 8
SWA_DH = 64
DILATED_BRANCHES = ((128, 1), (512, 4), (2048, 16))
SWA_BLOCK = 128
REL_BUCKETS = 32
REL_MAX_DIST = 2048
N_EXPERTS = 32
TOP_K = 4
D_FF = D_MODEL
SWIGLU_LIMIT = 7.0
SWIGLU_ALPHA = 1.702
MOE_BLOCK = 256

LANES = 128
GDN_W = GDN_HEADS * GDN_DK
SWA_W = SWA_HEADS * SWA_DH
SWA_SLABS = SWA_W // LANES
CONV_DIM = 3 * GDN_W
COL_QKV = 0
COL_GATE = COL_QKV + CONV_DIM
COL_BA = COL_GATE + GDN_W
COL_QB = COL_BA + LANES
COL_KB = COL_QB + SWA_W
COL_VB = COL_KB + SWA_W
IN_PACKED = COL_VB + SWA_W

VMEM_LIMIT = 56 * 1024 * 1024

NEG_INF = float("-inf")


def _cparams(sem, **kw):
    return pltpu.CompilerParams(dimension_semantics=sem, vmem_limit_bytes=VMEM_LIMIT, **kw)


def _bdot(a, b):
    return jnp.dot(a.astype(BF16), b.astype(BF16), preferred_element_type=F32)


def _bdot_nt(a, b):
    return lax.dot_general(a.astype(BF16), b.astype(BF16), (((1,), (1,)), ((), ())),
                           preferred_element_type=F32)


def _bdot_tn(a, b):
    return lax.dot_general(a.astype(BF16), b.astype(BF16), (((0,), (0,)), ((), ())),
                           preferred_element_type=F32)


def _split3(x):
    hi = x.astype(BF16)
    r1 = x - hi.astype(F32)
    mid = r1.astype(BF16)
    lo = (r1 - mid.astype(F32)).astype(BF16)
    return hi, mid, lo


def _dot_split3(a_bf16, x):
    return sum(jnp.dot(a_bf16, part, preferred_element_type=F32) for part in _split3(x))


def _silu(x):
    return x * jax.nn.sigmoid(x)


def _in_proj_body(x_ref, nw_ref, w_ref, qnw_ref, knw_ref, hsum_ref,
                  qkv_ref, gate_ref, ba_ref, qb_ref, kb_ref, vb_ref):
    x = x_ref[...]
    ms = jnp.mean(x * x, axis=-1, keepdims=True)
    xn = (x * lax.rsqrt(ms + EPS) * nw_ref[...]).astype(BF16)

    def seg(lo, width):
        return jnp.dot(xn, w_ref[:, lo:lo + width], preferred_element_type=F32)

    qkv_ref[...] = seg(COL_QKV, CONV_DIM)
    gate_ref[...] = seg(COL_GATE, GDN_W)
    ba_ref[...] = seg(COL_BA, LANES)

    def head_rmsnorm(z, w_row, scale):
        sq = z * z
        hi = sq.astype(BF16)
        lo = (sq - hi.astype(F32)).astype(BF16)
        hmean = (jnp.dot(hi, hsum_ref[...], preferred_element_type=F32)
                 + jnp.dot(lo, hsum_ref[...], preferred_element_type=F32))
        return z * lax.rsqrt(hmean + EPS) * w_row * scale

    def put_slabs(ref, z):
        for j in range(SWA_SLABS):
            ref[j] = z[:, j * LANES:(j + 1) * LANES]

    put_slabs(qb_ref, head_rmsnorm(seg(COL_QB, SWA_W), qnw_ref[...], SWA_DH ** -0.5))
    put_slabs(kb_ref, head_rmsnorm(seg(COL_KB, SWA_W), knw_ref[...], 1.0))
    put_slabs(vb_ref, seg(COL_VB, SWA_W))


def _in_proj(x2, norm1_w, w_packed, q_norm_w, k_norm_w, tm=512):
    t = x2.shape[0]
    hsum = jnp.asarray(np.kron(np.eye(SWA_HEADS), np.ones((SWA_DH, SWA_DH))) / SWA_DH, BF16)
    row = lambda i: (i, 0)
    full = lambda i: (0, 0)
    slab = lambda i: (0, i, 0)
    return pl.pallas_call(
        _in_proj_body,
        grid=(t // tm,),
        in_specs=[
            pl.BlockSpec((tm, D_MODEL), row),
            pl.BlockSpec((1, D_MODEL), full),
            pl.BlockSpec((D_MODEL, IN_PACKED), full),
            pl.BlockSpec((1, SWA_W), full),
            pl.BlockSpec((1, SWA_W), full),
            pl.BlockSpec((SWA_W, SWA_W), full),
        ],
        out_specs=[
            pl.BlockSpec((tm, CONV_DIM), row),
            pl.BlockSpec((tm, GDN_W), row),
            pl.BlockSpec((tm, LANES), row),
            pl.BlockSpec((SWA_SLABS, tm, LANES), slab),
            pl.BlockSpec((SWA_SLABS, tm, LANES), slab),
            pl.BlockSpec((SWA_SLABS, tm, LANES), slab),
        ],
        out_shape=[
            jax.ShapeDtypeStruct((t, CONV_DIM), F32),
            jax.ShapeDtypeStruct((t, GDN_W), F32),
            jax.ShapeDtypeStruct((t, LANES), F32),
            jax.ShapeDtypeStruct((SWA_SLABS, t, LANES), F32),
            jax.ShapeDtypeStruct((SWA_SLABS, t, LANES), F32),
            jax.ShapeDtypeStruct((SWA_SLABS, t, LANES), F32),
        ],
        compiler_params=_cparams(("parallel",)),
        name="in_proj",
    )(x2, norm1_w.reshape(1, D_MODEL), w_packed,
      jnp.tile(q_norm_w, SWA_HEADS).reshape(1, SWA_W), jnp.tile(k_norm_w, SWA_HEADS).reshape(1, SWA_W), hsum)


def _pack_w_in(w_in):
    o = np.cumsum([0, GDN_W, GDN_W, GDN_W, GDN_W, GDN_HEADS, GDN_HEADS, SWA_W, SWA_W, SWA_W])
    ba = jnp.pad(w_in[:, o[4]:o[6]], ((0, 0), (0, LANES - 2 * GDN_HEADS)))
    return jnp.concatenate([w_in[:, o[0]:o[4]], ba, w_in[:, o[6]:o[9]]], axis=1).astype(BF16)


def _unit_lower_inverses(a_list, n, blk):
    ii = lax.broadcasted_iota(I32, (n, n), 0)
    jj = lax.broadcasted_iota(I32, (n, n), 1)
    same = lambda sh: jnp.right_shift(ii, sh) == jnp.right_shift(jj, sh)
    eye = (ii == jj).astype(F32)
    pair = same(1)
    t_list = [eye - jnp.where(pair, a, 0.0) for a in a_list]
    sh = 1
    while (1 << sh) < blk:
        join = same(sh + 1) & jnp.logical_not(same(sh))
        te = [_bdot(t, jnp.where(join, a, 0.0)) for t, a in zip(t_list, a_list)]
        tet = [_bdot(x, t) for x, t in zip(te, t_list)]
        t_list = [t - y for t, y in zip(t_list, tet)]
        sh += 1
    return t_list


def _gdn_body(chunk, n_chunks, qkv_ref, gate_ref, ba_ref, convw_ref, alane_ref, dtlane_ref, normw_ref,
              o_ref, tail_ref, state_ref):
    c = chunk
    rows = chunk * n_chunks

    @pl.when(pl.program_id(1) == 0)
    def _():
        tail_ref[...] = jnp.zeros_like(tail_ref)
        state_ref[...] = jnp.zeros_like(state_ref)

    x = qkv_ref[0]
    xs = jnp.concatenate([tail_ref[...], x], axis=0)
    w = convw_ref[...]
    conv = (xs[5:5 + rows] * w[0:1] + xs[6:6 + rows] * w[1:2] + xs[7:7 + rows] * w[2:3] + x * w[3:4])
    tail_ref[...] = x[rows - 8:rows]
    y = _silu(conv)

    ba = ba_ref[0]
    beta_all = jax.nn.sigmoid(ba)
    z = ba + dtlane_ref[...]
    softplus = jnp.maximum(z, 0.0) + jnp.log(1.0 + jnp.exp(-jnp.abs(z)))
    g_all = -alane_ref[...] * softplus
    ri = lax.broadcasted_iota(I32, (rows, rows), 0)
    ci = lax.broadcasted_iota(I32, (rows, rows), 1)
    csh = c.bit_length() - 1
    tri = ((ri >= ci) & (jnp.right_shift(ri, csh) == jnp.right_shift(ci, csh))).astype(F32)
    gc_all = _dot_split3(tri.astype(BF16), g_all)

    heads = range(GDN_HEADS)
    qn, kn, vv = [], [], []
    for h in heads:
        lo = h * GDN_DK
        q = y[:, lo:lo + GDN_DK]
        k = y[:, GDN_W + lo:GDN_W + lo + GDN_DK]
        qn.append(q * lax.rsqrt(jnp.sum(q * q, axis=-1, keepdims=True) + EPS) * (GDN_DK ** -0.5))
        kn.append(k * lax.rsqrt(jnp.sum(k * k, axis=-1, keepdims=True) + EPS))
        vv.append(y[:, 2 * GDN_W + lo:2 * GDN_W + lo + GDN_DV])

    hs = GDN_HEADS * c
    hsh = csh
    ii = lax.broadcasted_iota(I32, (hs, hs), 0)
    jj = lax.broadcasted_iota(I32, (hs, hs), 1)
    same_head = jnp.right_shift(ii, hsh) == jnp.right_shift(jj, hsh)
    row_head = jnp.right_shift(lax.broadcasted_iota(I32, (hs, GDN_DV), 0), hsh)

    def stack(parts, r0):
        return jnp.concatenate([p[r0:r0 + c] for p in parts], axis=0)

    def col(src, lane0, r0):
        return jnp.concatenate([src[r0:r0 + c, lane0 + h:lane0 + h + 1] for h in heads], axis=0)

    def diag_blocks(p):
        out = jnp.where(row_head == 0, p[:, 0:GDN_DV], 0.0)
        for h in range(1, GDN_HEADS):
            out = out + jnp.where(row_head == h, p[:, h * GDN_DV:(h + 1) * GDN_DV], 0.0)
        return out

    q_st, u_rhs, kdec, cdecay, a_kk, a_qk = ([] for _ in range(6))
    for n in range(n_chunks):
        r0 = n * c
        k = stack(kn, r0)
        q = stack(qn, r0)
        v = stack(vv, r0)
        beta = col(beta_all, 0, r0)
        gcol = col(gc_all, GDN_HEADS, r0)
        grow = jnp.transpose(jnp.broadcast_to(gcol, (hs, LANES)))[0:1, :]
        glast = jnp.concatenate(
            [jnp.broadcast_to(gc_all[r0 + c - 1:r0 + c, GDN_HEADS + h:GDN_HEADS + h + 1], (c, 1)) for h in heads],
            axis=0)
        dec = jnp.exp(jnp.where(same_head & (ii >= jj), gcol - grow, NEG_INF))
        eg = jnp.exp(gcol)
        kb = k * beta
        prod = _bdot_nt(jnp.concatenate([kb, q], axis=0), k)
        a_kk.append(jnp.where(ii > jj, prod[:hs] * dec, 0.0))
        a_qk.append(prod[hs:] * dec)
        q_st.append(q * eg)
        u_rhs.append(jnp.concatenate([v * beta, kb * eg], axis=-1))
        kdec.append(k * jnp.exp(glast - gcol))
        cdecay.append(jnp.concatenate(
            [jnp.broadcast_to(jnp.exp(gc_all[r0 + c - 1:r0 + c, GDN_HEADS + h:GDN_HEADS + h + 1]), (1, GDN_DV))
             for h in heads], axis=-1))
    t_inv = _unit_lower_inverses(a_kk, hs, c)
    sols = [_bdot(t, r) for t, r in zip(t_inv, u_rhs)]

    gate = gate_ref[0]
    normw = normw_ref[...]
    s_cat = state_ref[...]
    for n in range(n_chunks):
        r0 = n * c
        u, wk = sols[n][:, :GDN_DV], sols[n][:, GDN_DV:]
        p = _bdot(jnp.concatenate([wk, q_st[n]], axis=0), s_cat)
        v_new = u - diag_blocks(p[:hs])
        o = diag_blocks(p[hs:]) + _bdot(a_qk[n], v_new)
        v_bd = jnp.concatenate([jnp.where(row_head == h, v_new, 0.0) for h in heads], axis=-1)
        s_cat = s_cat * cdecay[n] + _bdot_tn(kdec[n], v_bd)
        o = o * lax.rsqrt(jnp.mean(o * o, axis=-1, keepdims=True) + EPS) * normw
        for h in heads:
            lo = h * GDN_DV
            o_ref[0, r0:r0 + c, lo:lo + GDN_DV] = o[h * c:(h + 1) * c] * _silu(gate[r0:r0 + c, lo:lo + GDN_DV])
    state_ref[...] = s_cat


def _gdn(qkv, gate, ba, conv_w, a_log, dt_bias, gdn_norm_w, chunk=64, n_chunks=16):
    b, s, _ = qkv.shape
    rows = chunk * n_chunks
    pad = jnp.zeros((LANES - 2 * GDN_HEADS,), F32)
    alane = jnp.concatenate([jnp.zeros((GDN_HEADS,), F32), jnp.exp(a_log.astype(F32)), pad]).reshape(1, LANES)
    dtlane = jnp.concatenate([jnp.zeros((GDN_HEADS,), F32), dt_bias.astype(F32), pad]).reshape(1, LANES)
    blk = lambda i, j: (i, j, 0)
    full = lambda i, j: (0, 0)
    return pl.pallas_call(
        functools.partial(_gdn_body, chunk, n_chunks),
        grid=(b, s // rows),
        in_specs=[
            pl.BlockSpec((1, rows, CONV_DIM), blk),
            pl.BlockSpec((1, rows, GDN_W), blk),
            pl.BlockSpec((1, rows, LANES), blk),
            pl.BlockSpec((CONV_K, CONV_DIM), full),
            pl.BlockSpec((1, LANES), full),
            pl.BlockSpec((1, LANES), full),
            pl.BlockSpec((1, GDN_DV), full),
        ],
        out_specs=pl.BlockSpec((1, rows, GDN_W), blk),
        out_shape=jax.ShapeDtypeStruct((b, s, GDN_W), F32),
        scratch_shapes=[pltpu.VMEM((8, CONV_DIM), F32), pltpu.VMEM((GDN_DK, GDN_HEADS * GDN_DV), F32)],
        compiler_params=_cparams(("parallel", "arbitrary")),
        name="gdn",
    )(qkv, gate, ba, conv_w.astype(F32), alane, dtlane, gdn_norm_w.astype(F32).reshape(1, GDN_DV))


def _t5_bucket(dist):
    max_exact = REL_BUCKETS // 2
    n = np.maximum(dist, 0)
    large = max_exact + (np.log(np.maximum(n, 1) / max_exact) / math.log(REL_MAX_DIST / max_exact)
                         * (REL_BUCKETS - max_exact)).astype(np.int32)
    large = np.minimum(large, REL_BUCKETS - 1)
    return np.where(n < max_exact, n, large).astype(np.int32)


def _branch_bias(rel_bias, window, dilation):
    kj = np.arange(2 * SWA_BLOCK)[:, None]
    qi = np.arange(SWA_BLOCK)[None, :]
    steps = qi + SWA_BLOCK - kj
    valid = (steps >= 0) & (steps <= window // dilation)
    onehot = np.eye(REL_BUCKETS, dtype=np.float32)[_t5_bucket(steps * dilation).reshape(-1)]
    bias = jnp.dot(jnp.asarray(onehot), rel_bias.astype(F32), precision=lax.Precision.HIGHEST)
    bias = bias.reshape(2 * SWA_BLOCK, SWA_BLOCK, SWA_HEADS).transpose(2, 0, 1)
    bias = jnp.where(jnp.asarray(valid)[None], bias, NEG_INF)
    return jnp.concatenate([bias[:, :SWA_BLOCK], jnp.full((SWA_HEADS, SWA_BLOCK, SWA_BLOCK), NEG_INF, F32),
                            bias[:, SWA_BLOCK:]], axis=1)


SWA_SUPER = 2048
STAT_ROWS = 16


def _swa_body(q_ref, k_ref, v_ref, bias_ref, spread_ref, o_ref, acco, accl):
    sb = pl.program_id(1)
    n_qblk = SWA_SUPER // SWA_BLOCK
    stat_pad = jnp.zeros((LANES - SWA_HEADS, SWA_BLOCK), F32)
    last_branch = len(DILATED_BRANCHES) - 1

    for bi, (_, d) in enumerate(DILATED_BRANCHES):
        dsh = d.bit_length() - 1
        per_class = n_qblk // d

        def rows(start, d=d):
            if d > 1:
                return pl.ds(start, SWA_BLOCK, stride=d)
            return pl.ds(pl.multiple_of(start, SWA_BLOCK), SWA_BLOCK)

        def qblock(idx, carry, bi=bi, d=d, dsh=dsh, per_class=per_class, rows=rows):
            r = jnp.bitwise_and(idx, d - 1)
            n = jnp.right_shift(idx, dsh)
            loc = r + n * (d * SWA_BLOCK)
            cur = sb * SWA_SUPER + loc
            first = (sb * per_class + n) == 0
            prev = jnp.maximum(cur - d * SWA_BLOCK, r)
            poff = pl.multiple_of(jnp.where(first, SWA_BLOCK, 0), SWA_BLOCK)
            scores, values = [], []
            for j in range(SWA_SLABS):
                qb = q_ref[j, rows(loc), :].astype(BF16)
                kw = jnp.concatenate([k_ref[j, rows(prev), :], k_ref[j, rows(cur), :]], axis=0).astype(BF16)
                vw = jnp.concatenate([v_ref[j, rows(prev), :], v_ref[j, rows(cur), :]], axis=0).astype(BF16)
                for lo in range(0, LANES, SWA_DH):
                    scores.append(_bdot_nt(kw[:, lo:lo + SWA_DH], qb[:, lo:lo + SWA_DH]))
                    values.append(vw[:, lo:lo + SWA_DH])
            mxs, dens, probs = [], [], []
            for h in range(SWA_HEADS):
                bias = jnp.concatenate([bias_ref[bi, h, pl.ds(poff, SWA_BLOCK), :],
                                        bias_ref[bi, h, 2 * SWA_BLOCK:3 * SWA_BLOCK, :]], axis=0)
                st = scores[h] + bias
                mx = jnp.max(st, axis=0, keepdims=True)
                p = jnp.exp(st - mx)
                mxs.append(mx)
                dens.append(jnp.sum(p, axis=0, keepdims=True))
                probs.append(p.astype(BF16))
            outs = [_bdot_tn(values[h], probs[h]) for h in range(SWA_HEADS)]
            mx = jnp.concatenate(mxs, axis=0)
            den = jnp.concatenate(dens, axis=0)
            lse = mx + jnp.log(den)
            if bi == 0:
                scale = 1.0 / den
                new = lse
            else:
                lacc = jnp.transpose(accl[rows(loc), :])[0:SWA_HEADS]
                m2 = jnp.maximum(lacc, lse)
                ea = jnp.exp(lacc - m2)
                tot = ea + jnp.exp(lse - m2)
                scale = jnp.exp(mx - m2) / tot
                keep = ea / tot
                new = m2 + jnp.log(tot)
            o_tok = jnp.transpose(
                jnp.concatenate([outs[h] * scale[h:h + 1] for h in range(SWA_HEADS)], axis=0))
            if bi > 0:
                kpad = jnp.concatenate([keep, jnp.zeros((STAT_ROWS - SWA_HEADS, SWA_BLOCK), F32)], axis=0)
                hi = kpad.astype(BF16)
                lo2 = (kpad - hi.astype(F32)).astype(BF16)
                spread = spread_ref[...]
                keep_tok = _bdot_tn(hi, spread) + _bdot_tn(lo2, spread)
            for j in range(SWA_SLABS):
                o_slab = o_tok[:, j * LANES:(j + 1) * LANES]
                if bi > 0:
                    o_slab = acco[j, rows(loc), :] * keep_tok[:, j * LANES:(j + 1) * LANES] + o_slab
                acco[j, rows(loc), :] = o_slab
            if bi < last_branch:
                accl[rows(loc), :] = jnp.transpose(jnp.concatenate([new, stat_pad], axis=0))
            return carry

        lax.fori_loop(0, n_qblk, qblock, 0)
    for j in range(SWA_SLABS):
        o_ref[:, j * LANES:(j + 1) * LANES] = acco[j].astype(BF16)


def _dilated_attention(qb, kb, vb, rel_bias, b, s):
    bias = jnp.stack([_branch_bias(rel_bias, w, d) for w, d in DILATED_BRANCHES])
    spread = np.zeros((STAT_ROWS, SWA_W), np.float32)
    for h in range(SWA_HEADS):
        spread[h, h * SWA_DH:(h + 1) * SWA_DH] = 1.0
    once = pl.Buffered(1)
    spans = s // SWA_SUPER
    span = lambda i, j: (0, i * spans + j, 0)
    seq = lambda i, j: (0, i, 0)
    return pl.pallas_call(
        _swa_body,
        grid=(b, spans),
        in_specs=[
            pl.BlockSpec((SWA_SLABS, SWA_SUPER, LANES), span),
            pl.BlockSpec((SWA_SLABS, s, LANES), seq, pipeline_mode=once),
            pl.BlockSpec((SWA_SLABS, s, LANES), seq, pipeline_mode=once),
            pl.BlockSpec(bias.shape, lambda i, j: (0, 0, 0, 0), pipeline_mode=once),
            pl.BlockSpec((STAT_ROWS, SWA_W), lambda i, j: (0, 0)),
        ],
        out_specs=pl.BlockSpec((SWA_SUPER, SWA_W), lambda i, j: (i * spans + j, 0)),
        out_shape=jax.ShapeDtypeStruct((b * s, SWA_W), BF16),
        scratch_shapes=[pltpu.VMEM((SWA_SLABS, SWA_SUPER, LANES), F32), pltpu.VMEM((SWA_SUPER, LANES), F32)],
        compiler_params=_cparams(("parallel", "arbitrary")),
        name="swa",
    )(qb, kb, vb, bias, jnp.asarray(spread, BF16))


TOKEN_TILE = D_MODEL // LANES


def _put_token_tiles(ref, rows):
    m = rows.shape[0]
    for s in range(TOKEN_TILE):
        ref[pl.ds(s, m, stride=TOKEN_TILE), :] = rows[:, s * LANES:(s + 1) * LANES]


def _get_token_tiles(ref, m, base=0):
    return jnp.concatenate([ref[pl.ds(base + s, m, stride=TOKEN_TILE), :] for s in range(TOKEN_TILE)], axis=-1)


def _out_proj_body(x_ref, oa_ref, ob_ref, w_ref, nw_ref, wr_ref, br_ref, h_ref, hn_ref, e_ref, g_ref):
    h = (x_ref[...] + _bdot(oa_ref[...], w_ref[0:GDN_W, :]) + _bdot(ob_ref[...], w_ref[GDN_W:, :]))
    h_ref[...] = h
    hn = h * lax.rsqrt(jnp.mean(h * h, axis=-1, keepdims=True) + EPS) * nw_ref[...]
    _put_token_tiles(hn_ref, hn)
    hn_hi = hn.astype(BF16)
    hn_lo = (hn - hn_hi.astype(F32)).astype(BF16)
    logits = (jnp.dot(hn_hi, wr_ref[0], preferred_element_type=F32)
              + jnp.dot(hn_hi, wr_ref[1], preferred_element_type=F32)
              + jnp.dot(hn_lo, wr_ref[0], preferred_element_type=F32) + br_ref[...])
    lane = lax.broadcasted_iota(I32, logits.shape, 1)
    e_tile = jnp.zeros(logits.shape, I32)
    g_tile = jnp.zeros(logits.shape, F32)
    top0 = None
    den = None
    for k in range(TOP_K):
        mx = jnp.max(logits, axis=-1, keepdims=True)
        idx = jnp.min(jnp.where(logits == mx, lane, LANES), axis=-1, keepdims=True)
        logits = jnp.where(lane == idx, NEG_INF, logits)
        if k == 0:
            top0 = mx
        ex = jnp.exp(mx - top0)
        den = ex if k == 0 else den + ex
        e_tile = jnp.where(lane == k, idx, e_tile)
        g_tile = jnp.where(lane == k, ex, g_tile)
    e_ref[...] = e_tile
    g_ref[...] = g_tile / den


def _out_proj(x2, oa, ob, w_out, norm2_w, w_router, b_router, tm=512):
    t = x2.shape[0]
    wr = jnp.pad(w_router.astype(F32), ((0, 0), (0, LANES - N_EXPERTS)))
    wr_hi = wr.astype(BF16)
    wr = jnp.stack([wr_hi, (wr - wr_hi.astype(F32)).astype(BF16)])
    br = jnp.concatenate([b_router.astype(F32), jnp.full((LANES - N_EXPERTS,), NEG_INF, F32)]).reshape(1, LANES)
    row = lambda i: (i, 0)
    full = lambda i: (0, 0)
    return pl.pallas_call(
        _out_proj_body,
        grid=(t // tm,),
        in_specs=[
            pl.BlockSpec((tm, D_MODEL), row),
            pl.BlockSpec((tm, GDN_W), row),
            pl.BlockSpec((tm, SWA_W), row),
            pl.BlockSpec((GDN_W + SWA_W, D_MODEL), full),
            pl.BlockSpec((1, D_MODEL), full),
            pl.BlockSpec((2, D_MODEL, LANES), lambda i: (0, 0, 0)),
            pl.BlockSpec((1, LANES), full),
        ],
        out_specs=[
            pl.BlockSpec((tm, D_MODEL), row),
            pl.BlockSpec((tm * TOKEN_TILE, LANES), row),
            pl.BlockSpec((tm, LANES), row),
            pl.BlockSpec((tm, LANES), row),
        ],
        out_shape=[
            jax.ShapeDtypeStruct((t, D_MODEL), F32),
            jax.ShapeDtypeStruct((t * TOKEN_TILE, LANES), F32),
            jax.ShapeDtypeStruct((t, LANES), I32),
            jax.ShapeDtypeStruct((t, LANES), F32),
        ],
        compiler_params=_cparams(("parallel",)),
        name="out_proj",
    )(x2, oa, ob, w_out.astype(BF16), norm2_w.astype(F32).reshape(1, D_MODEL), wr, br)


def _moe_plan(top_e):
    t = top_e.shape[0]
    n = t * TOP_K
    p_rows = n + N_EXPERTS * MOE_BLOCK
    nblk = p_rows // MOE_BLOCK
    e_flat = top_e.reshape(-1)
    experts = jnp.arange(N_EXPERTS, dtype=I32)
    counts = jnp.sum((e_flat[:, None] == experts[None, :]).astype(I32), axis=0)
    padded = (counts + MOE_BLOCK - 1) // MOE_BLOCK * MOE_BLOCK
    pad_end = jnp.cumsum(padded)
    cand = jnp.arange(MOE_BLOCK, dtype=I32)[None, :] < (padded - counts)[:, None]
    cand_key = jnp.where(cand, 2 * experts[:, None] + 1, 2 * N_EXPERTS).reshape(-1)
    keys = jnp.concatenate([2 * e_flat, cand_key])
    id_bits = (p_rows - 1).bit_length()
    assert (2 * N_EXPERTS + 1) << id_bits < 2 ** 31
    packed = jnp.left_shift(keys, id_bits) | jnp.arange(p_rows, dtype=I32)
    src = jnp.bitwise_and(jnp.sort(packed), (1 << id_bits) - 1)
    is_pad = src >= n
    pad_rank = jnp.cumsum(is_pad.astype(I32)) - 1
    tok = jnp.right_shift(src, TOP_K.bit_length() - 1)
    gather_tok = jnp.where(is_pad, 0, tok)
    scatter_row = jnp.where(is_pad, n + pad_rank, jnp.bitwise_and(src, TOP_K - 1) * t + tok)
    blk_start = jnp.arange(nblk, dtype=I32) * MOE_BLOCK
    blk_expert = jnp.minimum(jnp.sum((blk_start[:, None] >= pad_end[None, :]).astype(I32), axis=1), N_EXPERTS - 1)
    return gather_tok.reshape(nblk, 1, MOE_BLOCK), scatter_row.reshape(nblk, 1, MOE_BLOCK), blk_expert


MOE_COL_CHUNK = 256


def _moe_body(bexp_ref, gidx_ref, sidx_ref, hn_ref, wgu_ref, bgu_ref, wdn_ref, bdn_ref, y_ref,
              xbuf, ybuf, wgu_bf, wdn_bf, gsem, ssem):
    i = pl.program_id(0)
    last = pl.num_programs(0) - 1
    slot = i % 2

    def gather_copy(tok, s, r):
        return pltpu.make_async_copy(hn_ref.at[tok], xbuf.at[s, pl.ds(r * TOKEN_TILE, TOKEN_TILE)], gsem.at[s])

    def scatter_copy(row, r):
        return pltpu.make_async_copy(ybuf.at[pl.ds(r * TOKEN_TILE, TOKEN_TILE)], y_ref.at[row], ssem)

    @pl.when(i == 0)
    def _():
        for r in range(MOE_BLOCK):
            gather_copy(gidx_ref[0, 0, r], 0, r).start()
        ybuf[...] = jnp.zeros_like(ybuf)

    @pl.when(i >= 0)
    def _():
        for r in range(MOE_BLOCK):
            scatter_copy(sidx_ref[0, 0, r], r).start(priority=r % 2)
            gather_copy(gidx_ref[1, 0, r], 1 - slot, r).start(priority=r % 2)

    @pl.when(jnp.logical_or(i == 0, bexp_ref[i] != bexp_ref[jnp.maximum(i - 1, 0)]))
    def _():
        wgu_bf[...] = wgu_ref[0].astype(BF16)
        wdn_bf[...] = wdn_ref[0].astype(BF16)

    for r in range(MOE_BLOCK):
        gather_copy(0, slot, r).wait()
    x = _get_token_tiles(xbuf.at[slot], MOE_BLOCK).astype(BF16)
    n_col = D_FF // MOE_COL_CHUNK
    acts = []
    for c in range(n_col):
        lo = c * MOE_COL_CHUNK
        g = jnp.dot(x, wgu_bf[:, lo:lo + MOE_COL_CHUNK], preferred_element_type=F32)
        u = jnp.dot(x, wgu_bf[:, D_FF + lo:D_FF + lo + MOE_COL_CHUNK], preferred_element_type=F32)
        g = jnp.minimum(g + bgu_ref[0, :, lo:lo + MOE_COL_CHUNK], SWIGLU_LIMIT)
        u = jnp.clip(u + bgu_ref[0, :, D_FF + lo:D_FF + lo + MOE_COL_CHUNK], -SWIGLU_LIMIT, SWIGLU_LIMIT)
        acts.append(((u + 1.0) * g * jax.nn.sigmoid(SWIGLU_ALPHA * g)).astype(BF16))
    act = jnp.concatenate(acts, axis=-1)
    outs = []
    for c in range(D_MODEL // MOE_COL_CHUNK):
        lo = c * MOE_COL_CHUNK
        outs.append(jnp.dot(act, wdn_bf[:, lo:lo + MOE_COL_CHUNK], preferred_element_type=F32)
                    + bdn_ref[0, :, lo:lo + MOE_COL_CHUNK])
    y = jnp.concatenate(outs, axis=-1)

    for r in range(MOE_BLOCK):
        scatter_copy(0, r).wait()
    _put_token_tiles(ybuf, y)

    @pl.when(i == last)
    def _():
        for r in range(MOE_BLOCK):
            scatter_copy(sidx_ref[1, 0, r], r).start()
        for r in range(MOE_BLOCK):
            scatter_copy(0, r).wait()
        for r in range(MOE_BLOCK):
            gather_copy(0, 1 - slot, r).wait()


def _moe_ffn(hn, plan, w_gu, b_gu, w_dn, b_dn):
    t = hn.shape[0] // TOKEN_TILE
    gather_tok, scatter_row, blk_expert = plan
    nblk = gather_tok.shape[0]
    n_rows = t * TOP_K + (N_EXPERTS + 1) * MOE_BLOCK
    gnext = jnp.concatenate([gather_tok[1:], gather_tok[-1:]], axis=0)
    gpair = jnp.concatenate([gather_tok, gnext], axis=1).reshape(nblk * 2, 1, MOE_BLOCK)
    before_first = (n_rows - MOE_BLOCK + jnp.arange(MOE_BLOCK, dtype=I32)).reshape(1, 1, MOE_BLOCK)
    sprev = jnp.concatenate([before_first, scatter_row[:-1]], axis=0)
    spair = jnp.concatenate([sprev, scatter_row], axis=1).reshape(nblk * 2, 1, MOE_BLOCK)
    wsel = lambda i, be: (be[i], 0, 0)
    pair = lambda i, be: (i, 0, 0)
    grid_spec = pltpu.PrefetchScalarGridSpec(
        num_scalar_prefetch=1,
        grid=(nblk,),
        in_specs=[
            pl.BlockSpec((2, 1, MOE_BLOCK), pair, memory_space=pltpu.SMEM),
            pl.BlockSpec((2, 1, MOE_BLOCK), pair, memory_space=pltpu.SMEM),
            pl.BlockSpec(memory_space=pl.ANY),
            pl.BlockSpec((1, D_MODEL, 2 * D_FF), wsel),
            pl.BlockSpec((1, 1, 2 * D_FF), wsel),
            pl.BlockSpec((1, D_FF, D_MODEL), wsel),
            pl.BlockSpec((1, 1, D_MODEL), wsel),
        ],
        out_specs=pl.BlockSpec(memory_space=pl.ANY),
        scratch_shapes=[
            pltpu.VMEM((2, MOE_BLOCK * TOKEN_TILE, LANES), F32),
            pltpu.VMEM((MOE_BLOCK * TOKEN_TILE, LANES), F32),
            pltpu.VMEM((D_MODEL, 2 * D_FF), BF16),
            pltpu.VMEM((D_FF, D_MODEL), BF16),
            pltpu.SemaphoreType.DMA((2,)),
            pltpu.SemaphoreType.DMA(()),
        ],
    )
    return pl.pallas_call(
        _moe_body,
        grid_spec=grid_spec,
        out_shape=jax.ShapeDtypeStruct((n_rows, TOKEN_TILE, LANES), F32),
        compiler_params=_cparams(("arbitrary",)),
        name="moe_ffn",
    )(blk_expert, gpair, spair, hn.reshape(t, TOKEN_TILE, LANES),
      w_gu.astype(F32), b_gu.astype(F32).reshape(N_EXPERTS, 1, 2 * D_FF),
      w_dn.astype(F32), b_dn.astype(F32).reshape(N_EXPERTS, 1, D_MODEL))


def _combine_body(h_ref, g_ref, *refs):
    y_refs, o_ref = refs[:TOP_K], refs[TOP_K]
    g = g_ref[...]
    m = g.shape[0]
    moe = g[:, 0:1] * _get_token_tiles(y_refs[0], m)
    for k in range(1, TOP_K):
        moe = moe + g[:, k:k + 1] * _get_token_tiles(y_refs[k], m)
    o_ref[...] = h_ref[...] + moe


def _combine(h, y_slots, gates, tm=512):
    t = h.shape[0]
    y2 = y_slots.reshape(-1, LANES)
    row = lambda i: (i, 0)
    kth = lambda k: (lambda i: (k * (t // tm) + i, 0))
    return pl.pallas_call(
        _combine_body,
        grid=(t // tm,),
        in_specs=[pl.BlockSpec((tm, D_MODEL), row), pl.BlockSpec((tm, LANES), row)]
        + [pl.BlockSpec((tm * TOKEN_TILE, LANES), kth(k)) for k in range(TOP_K)],
        out_specs=pl.BlockSpec((tm, D_MODEL), row),
        out_shape=jax.ShapeDtypeStruct((t, D_MODEL), F32),
        compiler_params=_cparams(("parallel",)),
        name="combine",
    )(h, gates, *([y2] * TOP_K))


def _layer(x, norm1_w, w_in, conv_w, a_log, dt_bias, gdn_norm_w, q_norm_w, k_norm_w, rel_bias,
           w_out, norm2_w, w_router, b_router, w_gate_up, b_gate_up, w_down, b_down):
    b, s, d = x.shape
    t = b * s
    x2 = x.reshape(t, d)
    qkv, gate, ba, qb, kb, vb = _in_proj(x2, norm1_w.astype(F32), _pack_w_in(w_in),
                                         q_norm_w.astype(F32), k_norm_w.astype(F32))
    oa = _gdn(qkv.reshape(b, s, CONV_DIM), gate.reshape(b, s, GDN_W), ba.reshape(b, s, LANES),
              conv_w, a_log, dt_bias, gdn_norm_w)
    ob = _dilated_attention(qb, kb, vb, rel_bias, b, s)
    h, hn, top_e, gates = _out_proj(x2, oa.reshape(t, GDN_W), ob, w_out, norm2_w, w_router, b_router)
    plan = _moe_plan(top_e[:, :TOP_K])
    y_slots = _moe_ffn(hn, plan, w_gate_up, b_gate_up, w_down, b_down)
    return _combine(h, y_slots, gates).reshape(b, s, d)


def kernel(x, norm1_w, w_in, conv_w, a_log, dt_bias, gdn_norm_w, q_norm_w, k_norm_w, rel_bias, w_out, norm2_w,
           w_router, b_router, w_gate_up, b_gate_up, w_down, b_down):
    h = x
    for l in range(norm1_w.shape[0]):
        h = _layer(h, norm1_w[l], w_in[l], conv_w[l], a_log[l], dt_bias[l], gdn_norm_w[l], q_norm_w[l],
                   k_norm_w[l], rel_bias, w_out[l], norm2_w[l], w_router[l], b_router[l], w_gate_up[l],
                   b_gate_up[l], w_down[l], b_down[l])
    return h
```

```python
import functools
import math

import numpy as np
import jax
import jax.numpy as jnp
from jax import lax
from jax.experimental import pallas as pl
from jax.experimental.pallas import tpu as pltpu

F32 = jnp.float32
BF16 = jnp.bfloat16
I32 = jnp.int32

EPS = 1e-6
D_MODEL = 1024
GDN_HEADS = 4
GDN_DK = 128
GDN_DV = 128
CONV_K = 4
SWA_HEADS = 8
SWA_DH = 64
DILATED_BRANCHES = ((128, 1), (512, 4), (2048, 16))
SWA_BLOCK = 128
REL_BUCKETS = 32
REL_MAX_DIST = 2048
N_EXPERTS = 32
TOP_K = 4
D_FF = D_MODEL
SWIGLU_LIMIT = 7.0
SWIGLU_ALPHA = 1.702
MOE_BLOCK = 256

LANES = 128
GDN_W = GDN_HEADS * GDN_DK
SWA_W = SWA_HEADS * SWA_DH
SWA_SLABS = SWA_W // LANES
CONV_DIM = 3 * GDN_W
COL_QKV = 0
COL_GATE = COL_QKV + CONV_DIM
COL_BA = COL_GATE + GDN_W
COL_QB = COL_BA + LANES
COL_KB = COL_QB + SWA_W
COL_VB = COL_KB + SWA_W
IN_PACKED = COL_VB + SWA_W

VMEM_LIMIT = 56 * 1024 * 1024

NEG_INF = float("-inf")


def _cparams(sem, **kw):
    return pltpu.CompilerParams(dimension_semantics=sem, vmem_limit_bytes=VMEM_LIMIT, **kw)


def _bdot(a, b):
    return jnp.dot(a.astype(BF16), b.astype(BF16), preferred_element_type=F32)


def _bdot_nt(a, b):
    return lax.dot_general(a.astype(BF16), b.astype(BF16), (((1,), (1,)), ((), ())),
                           preferred_element_type=F32)


def _bdot_tn(a, b):
    return lax.dot_general(a.astype(BF16), b.astype(BF16), (((0,), (0,)), ((), ())),
                           preferred_element_type=F32)


def _split3(x):
    hi = x.astype(BF16)
    r1 = x - hi.astype(F32)
    mid = r1.astype(BF16)
    lo = (r1 - mid.astype(F32)).astype(BF16)
    return hi, mid, lo


def _dot_split3(a_bf16, x):
    return sum(jnp.dot(a_bf16, part, preferred_element_type=F32) for part in _split3(x))


def _silu(x):
    return x * jax.nn.sigmoid(x)


def _in_proj_body(x_ref, nw_ref, w_ref, qnw_ref, knw_ref, hsum_ref,
                  qkv_ref, gate_ref, ba_ref, qb_ref, kb_ref, vb_ref):
    x = x_ref[...]
    ms = jnp.mean(x * x, axis=-1, keepdims=True)
    xn = (x * lax.rsqrt(ms + EPS) * nw_ref[...]).astype(BF16)

    def seg(lo, width):
        return jnp.dot(xn, w_ref[:, lo:lo + width], preferred_element_type=F32)

    qkv_ref[...] = seg(COL_QKV, CONV_DIM)
    gate_ref[...] = seg(COL_GATE, GDN_W)
    ba_ref[...] = seg(COL_BA, LANES)

    def head_rmsnorm(z, w_row, scale):
        sq = z * z
        hi = sq.astype(BF16)
        lo = (sq - hi.astype(F32)).astype(BF16)
        hmean = (jnp.dot(hi, hsum_ref[...], preferred_element_type=F32)
                 + jnp.dot(lo, hsum_ref[...], preferred_element_type=F32))
        return z * lax.rsqrt(hmean + EPS) * w_row * scale

    def put_slabs(ref, z):
        for j in range(SWA_SLABS):
            ref[j] = z[:, j * LANES:(j + 1) * LANES]

    put_slabs(qb_ref, head_rmsnorm(seg(COL_QB, SWA_W), qnw_ref[...], SWA_DH ** -0.5))
    put_slabs(kb_ref, head_rmsnorm(seg(COL_KB, SWA_W), knw_ref[...], 1.0))
    put_slabs(vb_ref, seg(COL_VB, SWA_W))


def _in_proj(x2, norm1_w, w_packed, q_norm_w, k_norm_w, tm=512):
    t = x2.shape[0]
    hsum = jnp.asarray(np.kron(np.eye(SWA_HEADS), np.ones((SWA_DH, SWA_DH))) / SWA_DH, BF16)
    row = lambda i: (i, 0)
    full = lambda i: (0, 0)
    slab = lambda i: (0, i, 0)
    return pl.pallas_call(
        _in_proj_body,
        grid=(t // tm,),
        in_specs=[
            pl.BlockSpec((tm, D_MODEL), row),
            pl.BlockSpec((1, D_MODEL), full),
            pl.BlockSpec((D_MODEL, IN_PACKED), full),
            pl.BlockSpec((1, SWA_W), full),
            pl.BlockSpec((1, SWA_W), full),
            pl.BlockSpec((SWA_W, SWA_W), full),
        ],
        out_specs=[
            pl.BlockSpec((tm, CONV_DIM), row),
            pl.BlockSpec((tm, GDN_W), row),
            pl.BlockSpec((tm, LANES), row),
            pl.BlockSpec((SWA_SLABS, tm, LANES), slab),
            pl.BlockSpec((SWA_SLABS, tm, LANES), slab),
            pl.BlockSpec((SWA_SLABS, tm, LANES), slab),
        ],
        out_shape=[
            jax.ShapeDtypeStruct((t, CONV_DIM), F32),
            jax.ShapeDtypeStruct((t, GDN_W), F32),
            jax.ShapeDtypeStruct((t, LANES), F32),
            jax.ShapeDtypeStruct((SWA_SLABS, t, LANES), F32),
            jax.ShapeDtypeStruct((SWA_SLABS, t, LANES), F32),
            jax.ShapeDtypeStruct((SWA_SLABS, t, LANES), F32),
        ],
        compiler_params=_cparams(("parallel",)),
        name="in_proj",
    )(x2, norm1_w.reshape(1, D_MODEL), w_packed,
      jnp.tile(q_norm_w, SWA_HEADS).reshape(1, SWA_W), jnp.tile(k_norm_w, SWA_HEADS).reshape(1, SWA_W), hsum)


def _pack_w_in(w_in):
    o = np.cumsum([0, GDN_W, GDN_W, GDN_W, GDN_W, GDN_HEADS, GDN_HEADS, SWA_W, SWA_W, SWA_W])
    ba = jnp.pad(w_in[:, o[4]:o[6]], ((0, 0), (0, LANES - 2 * GDN_HEADS)))
    return jnp.concatenate([w_in[:, o[0]:o[4]], ba, w_in[:, o[6]:o[9]]], axis=1).astype(BF16)


def _unit_lower_inverses(a_list, n, blk):
    ii = lax.broadcasted_iota(I32, (n, n), 0)
    jj = lax.broadcasted_iota(I32, (n, n), 1)
    same = lambda sh: jnp.right_shift(ii, sh) == jnp.right_shift(jj, sh)
    eye = (ii == jj).astype(F32)
    pair = same(1)
    t_list = [eye - jnp.where(pair, a, 0.0) for a in a_list]
    sh = 1
    while (1 << sh) < blk:
        join = same(sh + 1) & jnp.logical_not(same(sh))
        te = [_bdot(t, jnp.where(join, a, 0.0)) for t, a in zip(t_list, a_list)]
        tet = [_bdot(x, t) for x, t in zip(te, t_list)]
        t_list = [t - y for t, y in zip(t_list, tet)]
        sh += 1
    return t_list


def _gdn_body(chunk, n_chunks, qkv_ref, gate_ref, ba_ref, convw_ref, alane_ref, dtlane_ref, normw_ref,
              o_ref, tail_ref, state_ref):
    c = chunk
    rows = chunk * n_chunks

    @pl.when(pl.program_id(1) == 0)
    def _():
        tail_ref[...] = jnp.zeros_like(tail_ref)
        state_ref[...] = jnp.zeros_like(state_ref)

    x = qkv_ref[0]
    xs = jnp.concatenate([tail_ref[...], x], axis=0)
    w = convw_ref[...]
    conv = (xs[5:5 + rows] * w[0:1] + xs[6:6 + rows] * w[1:2] + xs[7:7 + rows] * w[2:3] + x * w[3:4])
    tail_ref[...] = x[rows - 8:rows]
    y = _silu(conv)

    ba = ba_ref[0]
    beta_all = jax.nn.sigmoid(ba)
    z = ba + dtlane_ref[...]
    softplus = jnp.maximum(z, 0.0) + jnp.log(1.0 + jnp.exp(-jnp.abs(z)))
    g_all = -alane_ref[...] * softplus
    ri = lax.broadcasted_iota(I32, (rows, rows), 0)
    ci = lax.broadcasted_iota(I32, (rows, rows), 1)
    csh = c.bit_length() - 1
    tri = ((ri >= ci) & (jnp.right_shift(ri, csh) == jnp.right_shift(ci, csh))).astype(F32)
    gc_all = _dot_split3(tri.astype(BF16), g_all)

    heads = range(GDN_HEADS)
    qn, kn, vv = [], [], []
    for h in heads:
        lo = h * GDN_DK
        q = y[:, lo:lo + GDN_DK]
        k = y[:, GDN_W + lo:GDN_W + lo + GDN_DK]
        qn.append(q * lax.rsqrt(jnp.sum(q * q, axis=-1, keepdims=True) + EPS) * (GDN_DK ** -0.5))
        kn.append(k * lax.rsqrt(jnp.sum(k * k, axis=-1, keepdims=True) + EPS))
        vv.append(y[:, 2 * GDN_W + lo:2 * GDN_W + lo + GDN_DV])

    hs = GDN_HEADS * c
    hsh = csh
    ii = lax.broadcasted_iota(I32, (hs, hs), 0)
    jj = lax.broadcasted_iota(I32, (hs, hs), 1)
    same_head = jnp.right_shift(ii, hsh) == jnp.right_shift(jj, hsh)
    row_head = jnp.right_shift(lax.broadcasted_iota(I32, (hs, GDN_DV), 0), hsh)

    def stack(parts, r0):
        return jnp.concatenate([p[r0:r0 + c] for p in parts], axis=0)

    def col(src, lane0, r0):
        return jnp.concatenate([src[r0:r0 + c, lane0 + h:lane0 + h + 1] for h in heads], axis=0)

    def diag_blocks(p):
        out = jnp.where(row_head == 0, p[:, 0:GDN_DV], 0.0)
        for h in range(1, GDN_HEADS):
            out = out + jnp.where(row_head == h, p[:, h * GDN_DV:(h + 1) * GDN_DV], 0.0)
        return out

    q_st, u_rhs, kdec, cdecay, a_kk, a_qk = ([] for _ in range(6))
    for n in range(n_chunks):
        r0 = n * c
        k = stack(kn, r0)
        q = stack(qn, r0)
        v = stack(vv, r0)
        beta = col(beta_all, 0, r0)
        gcol = col(gc_all, GDN_HEADS, r0)
        grow = jnp.transpose(jnp.broadcast_to(gcol, (hs, LANES)))[0:1, :]
        glast = jnp.concatenate(
            [jnp.broadcast_to(gc_all[r0 + c - 1:r0 + c, GDN_HEADS + h:GDN_HEADS + h + 1], (c, 1)) for h in heads],
            axis=0)
        dec = jnp.exp(jnp.where(same_head & (ii >= jj), gcol - grow, NEG_INF))
        eg = jnp.exp(gcol)
        kb = k * beta
        prod = _bdot_nt(jnp.concatenate([kb, q], axis=0), k)
        a_kk.append(jnp.where(ii > jj, prod[:hs] * dec, 0.0))
        a_qk.append(prod[hs:] * dec)
        q_st.append(q * eg)
        u_rhs.append(jnp.concatenate([v * beta, kb * eg], axis=-1))
        kdec.append(k * jnp.exp(glast - gcol))
        cdecay.append(jnp.concatenate(
            [jnp.broadcast_to(jnp.exp(gc_all[r0 + c - 1:r0 + c, GDN_HEADS + h:GDN_HEADS + h + 1]), (1, GDN_DV))
             for h in heads], axis=-1))
    t_inv = _unit_lower_inverses(a_kk, hs, c)
    sols = [_bdot(t, r) for t, r in zip(t_inv, u_rhs)]

    gate = gate_ref[0]
    normw = normw_ref[...]
    s_cat = state_ref[...]
    for n in range(n_chunks):
        r0 = n * c
        u, wk = sols[n][:, :GDN_DV], sols[n][:, GDN_DV:]
        p = _bdot(jnp.concatenate([wk, q_st[n]], axis=0), s_cat)
        v_new = u - diag_blocks(p[:hs])
        o = diag_blocks(p[hs:]) + _bdot(a_qk[n], v_new)
        v_bd = jnp.concatenate([jnp.where(row_head == h, v_new, 0.0) for h in heads], axis=-1)
        s_cat = s_cat * cdecay[n] + _bdot_tn(kdec[n], v_bd)
        o = o * lax.rsqrt(jnp.mean(o * o, axis=-1, keepdims=True) + EPS) * normw
        for h in heads:
            lo = h * GDN_DV
            o_ref[0, r0:r0 + c, lo:lo + GDN_DV] = o[h * c:(h + 1) * c] * _silu(gate[r0:r0 + c, lo:lo + GDN_DV])
    state_ref[...] = s_cat


def _gdn(qkv, gate, ba, conv_w, a_log, dt_bias, gdn_norm_w, chunk=64, n_chunks=8):
    b, s, _ = qkv.shape
    rows = chunk * n_chunks
    pad = jnp.zeros((LANES - 2 * GDN_HEADS,), F32)
    alane = jnp.concatenate([jnp.zeros((GDN_HEADS,), F32), jnp.exp(a_log.astype(F32)), pad]).reshape(1, LANES)
    dtlane = jnp.concatenate([jnp.zeros((GDN_HEADS,), F32), dt_bias.astype(F32), pad]).reshape(1, LANES)
    blk = lambda i, j: (i, j, 0)
    full = lambda i, j: (0, 0)
    return pl.pallas_call(
        functools.partial(_gdn_body, chunk, n_chunks),
        grid=(b, s // rows),
        in_specs=[
            pl.BlockSpec((1, rows, CONV_DIM), blk),
            pl.BlockSpec((1, rows, GDN_W), blk),
            pl.BlockSpec((1, rows, LANES), blk),
            pl.BlockSpec((CONV_K, CONV_DIM), full),
            pl.BlockSpec((1, LANES), full),
            pl.BlockSpec((1, LANES), full),
            pl.BlockSpec((1, GDN_DV), full),
        ],
        out_specs=pl.BlockSpec((1, rows, GDN_W), blk),
        out_shape=jax.ShapeDtypeStruct((b, s, GDN_W), F32),
        scratch_shapes=[pltpu.VMEM((8, CONV_DIM), F32), pltpu.VMEM((GDN_DK, GDN_HEADS * GDN_DV), F32)],
        compiler_params=_cparams(("parallel", "arbitrary")),
        name="gdn",
    )(qkv, gate, ba, conv_w.astype(F32), alane, dtlane, gdn_norm_w.astype(F32).reshape(1, GDN_DV))


def _t5_bucket(dist):
    max_exact = REL_BUCKETS // 2
    n = np.maximum(dist, 0)
    large = max_exact + (np.log(np.maximum(n, 1) / max_exact) / math.log(REL_MAX_DIST / max_exact)
                         * (REL_BUCKETS - max_exact)).astype(np.int32)
    large = np.minimum(large, REL_BUCKETS - 1)
    return np.where(n < max_exact, n, large).astype(np.int32)


def _branch_bias(rel_bias, window, dilation):
    kj = np.arange(2 * SWA_BLOCK)[:, None]
    qi = np.arange(SWA_BLOCK)[None, :]
    steps = qi + SWA_BLOCK - kj
    valid = (steps >= 0) & (steps <= window // dilation)
    onehot = np.eye(REL_BUCKETS, dtype=np.float32)[_t5_bucket(steps * dilation).reshape(-1)]
    bias = jnp.dot(jnp.asarray(onehot), rel_bias.astype(F32), precision=lax.Precision.HIGHEST)
    bias = bias.reshape(2 * SWA_BLOCK, SWA_BLOCK, SWA_HEADS).transpose(2, 0, 1)
    bias = jnp.where(jnp.asarray(valid)[None], bias, NEG_INF)
    return jnp.concatenate([bias[:, :SWA_BLOCK], jnp.full((SWA_HEADS, SWA_BLOCK, SWA_BLOCK), NEG_INF, F32),
                            bias[:, SWA_BLOCK:]], axis=1)


SWA_SUPER = 2048
STAT_ROWS = 16


def _swa_body(q_ref, k_ref, v_ref, bias_ref, spread_ref, o_ref, acco, accl):
    sb = pl.program_id(1)
    n_qblk = SWA_SUPER // SWA_BLOCK
    stat_pad = jnp.zeros((LANES - SWA_HEADS, SWA_BLOCK), F32)
    last_branch = len(DILATED_BRANCHES) - 1

    for bi, (_, d) in enumerate(DILATED_BRANCHES):
        dsh = d.bit_length() - 1
        per_class = n_qblk // d

        def rows(start, d=d):
            if d > 1:
                return pl.ds(start, SWA_BLOCK, stride=d)
            return pl.ds(pl.multiple_of(start, SWA_BLOCK), SWA_BLOCK)

        def qblock(idx, carry, bi=bi, d=d, dsh=dsh, per_class=per_class, rows=rows):
            r = jnp.bitwise_and(idx, d - 1)
            n = jnp.right_shift(idx, dsh)
            loc = r + n * (d * SWA_BLOCK)
            cur = sb * SWA_SUPER + loc
            first = (sb * per_class + n) == 0
            prev = jnp.maximum(cur - d * SWA_BLOCK, r)
            poff = pl.multiple_of(jnp.where(first, SWA_BLOCK, 0), SWA_BLOCK)
            scores, values = [], []
            for j in range(SWA_SLABS):
                qb = q_ref[j, rows(loc), :].astype(BF16)
                kw = jnp.concatenate([k_ref[j, rows(prev), :], k_ref[j, rows(cur), :]], axis=0).astype(BF16)
                vw = jnp.concatenate([v_ref[j, rows(prev), :], v_ref[j, rows(cur), :]], axis=0).astype(BF16)
                for lo in range(0, LANES, SWA_DH):
                    scores.append(_bdot_nt(kw[:, lo:lo + SWA_DH], qb[:, lo:lo + SWA_DH]))
                    values.append(vw[:, lo:lo + SWA_DH])
            mxs, dens, probs = [], [], []
            for h in range(SWA_HEADS):
                bias = jnp.concatenate([bias_ref[bi, h, pl.ds(poff, SWA_BLOCK), :],
                                        bias_ref[bi, h, 2 * SWA_BLOCK:3 * SWA_BLOCK, :]], axis=0)
                st = scores[h] + bias
                mx = jnp.max(st, axis=0, keepdims=True)
                p = jnp.exp(st - mx)
                mxs.append(mx)
                dens.append(jnp.sum(p, axis=0, keepdims=True))
                probs.append(p.astype(BF16))
            outs = [_bdot_tn(values[h], probs[h]) for h in range(SWA_HEADS)]
            mx = jnp.concatenate(mxs, axis=0)
            den = jnp.concatenate(dens, axis=0)
            lse = mx + jnp.log(den)
            if bi == 0:
                scale = 1.0 / den
                new = lse
            else:
                lacc = jnp.transpose(accl[rows(loc), :])[0:SWA_HEADS]
                m2 = jnp.maximum(lacc, lse)
                ea = jnp.exp(lacc - m2)
                tot = ea + jnp.exp(lse - m2)
                scale = jnp.exp(mx - m2) / tot
                keep = ea / tot
                new = m2 + jnp.log(tot)
            o_tok = jnp.transpose(
                jnp.concatenate([outs[h] * scale[h:h + 1] for h in range(SWA_HEADS)], axis=0))
            if bi > 0:
                kpad = jnp.concatenate([keep, jnp.zeros((STAT_ROWS - SWA_HEADS, SWA_BLOCK), F32)], axis=0)
                hi = kpad.astype(BF16)
                lo2 = (kpad - hi.astype(F32)).astype(BF16)
                spread = spread_ref[...]
                keep_tok = _bdot_tn(hi, spread) + _bdot_tn(lo2, spread)
            for j in range(SWA_SLABS):
                o_slab = o_tok[:, j * LANES:(j + 1) * LANES]
                if bi > 0:
                    o_slab = acco[j, rows(loc), :] * keep_tok[:, j * LANES:(j + 1) * LANES] + o_slab
                acco[j, rows(loc), :] = o_slab
            if bi < last_branch:
                accl[rows(loc), :] = jnp.transpose(jnp.concatenate([new, stat_pad], axis=0))
            return carry

        lax.fori_loop(0, n_qblk, qblock, 0, unroll=2)
    for j in range(SWA_SLABS):
        o_ref[:, j * LANES:(j + 1) * LANES] = acco[j].astype(BF16)


def _dilated_attention(qb, kb, vb, rel_bias, b, s):
    bias = jnp.stack([_branch_bias(rel_bias, w, d) for w, d in DILATED_BRANCHES])
    spread = np.zeros((STAT_ROWS, SWA_W), np.float32)
    for h in range(SWA_HEADS):
        spread[h, h * SWA_DH:(h + 1) * SWA_DH] = 1.0
    once = pl.Buffered(1)
    spans = s // SWA_SUPER
    span = lambda i, j: (0, i * spans + j, 0)
    seq = lambda i, j: (0, i, 0)
    return pl.pallas_call(
        _swa_body,
        grid=(b, spans),
        in_specs=[
            pl.BlockSpec((SWA_SLABS, SWA_SUPER, LANES), span),
            pl.BlockSpec((SWA_SLABS, s, LANES), seq, pipeline_mode=once),
            pl.BlockSpec((SWA_SLABS, s, LANES), seq, pipeline_mode=once),
            pl.BlockSpec(bias.shape, lambda i, j: (0, 0, 0, 0), pipeline_mode=once),
            pl.BlockSpec((STAT_ROWS, SWA_W), lambda i, j: (0, 0)),
        ],
        out_specs=pl.BlockSpec((SWA_SUPER, SWA_W), lambda i, j: (i * spans + j, 0)),
        out_shape=jax.ShapeDtypeStruct((b * s, SWA_W), BF16),
        scratch_shapes=[pltpu.VMEM((SWA_SLABS, SWA_SUPER, LANES), F32), pltpu.VMEM((SWA_SUPER, LANES), F32)],
        compiler_params=_cparams(("parallel", "arbitrary")),
        name="swa",
    )(qb, kb, vb, bias, jnp.asarray(spread, BF16))


TOKEN_TILE = D_MODEL // LANES


def _put_token_tiles(ref, rows):
    m = rows.shape[0]
    for s in range(TOKEN_TILE):
        ref[pl.ds(s, m, stride=TOKEN_TILE), :] = rows[:, s * LANES:(s + 1) * LANES]


def _get_token_tiles(ref, m, base=0):
    return jnp.concatenate([ref[pl.ds(base + s, m, stride=TOKEN_TILE), :] for s in range(TOKEN_TILE)], axis=-1)


def _out_proj_body(x_ref, oa_ref, ob_ref, w_ref, nw_ref, wr_ref, br_ref, h_ref, hn_ref, e_ref, g_ref):
    h = (x_ref[...] + _bdot(oa_ref[...], w_ref[0:GDN_W, :]) + _bdot(ob_ref[...], w_ref[GDN_W:, :]))
    h_ref[...] = h
    hn = h * lax.rsqrt(jnp.mean(h * h, axis=-1, keepdims=True) + EPS) * nw_ref[...]
    _put_token_tiles(hn_ref, hn)
    hn_hi = hn.astype(BF16)
    hn_lo = (hn - hn_hi.astype(F32)).astype(BF16)
    logits = (jnp.dot(hn_hi, wr_ref[0], preferred_element_type=F32)
              + jnp.dot(hn_hi, wr_ref[1], preferred_element_type=F32)
              + jnp.dot(hn_lo, wr_ref[0], preferred_element_type=F32) + br_ref[...])
    lane = lax.broadcasted_iota(I32, logits.shape, 1)
    e_tile = jnp.zeros(logits.shape, I32)
    g_tile = jnp.zeros(logits.shape, F32)
    top0 = None
    den = None
    for k in range(TOP_K):
        mx = jnp.max(logits, axis=-1, keepdims=True)
        idx = jnp.min(jnp.where(logits == mx, lane, LANES), axis=-1, keepdims=True)
        logits = jnp.where(lane == idx, NEG_INF, logits)
        if k == 0:
            top0 = mx
        ex = jnp.exp(mx - top0)
        den = ex if k == 0 else den + ex
        e_tile = jnp.where(lane == k, idx, e_tile)
        g_tile = jnp.where(lane == k, ex, g_tile)
    e_ref[...] = e_tile
    g_ref[...] = g_tile / den


def _out_proj(x2, oa, ob, w_out, norm2_w, w_router, b_router, tm=512):
    t = x2.shape[0]
    wr = jnp.pad(w_router.astype(F32), ((0, 0), (0, LANES - N_EXPERTS)))
    wr_hi = wr.astype(BF16)
    wr = jnp.stack([wr_hi, (wr - wr_hi.astype(F32)).astype(BF16)])
    br = jnp.concatenate([b_router.astype(F32), jnp.full((LANES - N_EXPERTS,), NEG_INF, F32)]).reshape(1, LANES)
    row = lambda i: (i, 0)
    full = lambda i: (0, 0)
    return pl.pallas_call(
        _out_proj_body,
        grid=(t // tm,),
        in_specs=[
            pl.BlockSpec((tm, D_MODEL), row),
            pl.BlockSpec((tm, GDN_W), row),
            pl.BlockSpec((tm, SWA_W), row),
            pl.BlockSpec((GDN_W + SWA_W, D_MODEL), full),
            pl.BlockSpec((1, D_MODEL), full),
            pl.BlockSpec((2, D_MODEL, LANES), lambda i: (0, 0, 0)),
            pl.BlockSpec((1, LANES), full),
        ],
        out_specs=[
            pl.BlockSpec((tm, D_MODEL), row),
            pl.BlockSpec((tm * TOKEN_TILE, LANES), row),
            pl.BlockSpec((tm, LANES), row),
            pl.BlockSpec((tm, LANES), row),
        ],
        out_shape=[
            jax.ShapeDtypeStruct((t, D_MODEL), F32),
            jax.ShapeDtypeStruct((t * TOKEN_TILE, LANES), F32),
            jax.ShapeDtypeStruct((t, LANES), I32),
            jax.ShapeDtypeStruct((t, LANES), F32),
        ],
        compiler_params=_cparams(("parallel",)),
        name="out_proj",
    )(x2, oa, ob, w_out.astype(BF16), norm2_w.astype(F32).reshape(1, D_MODEL), wr, br)


def _moe_plan(top_e):
    t = top_e.shape[0]
    n = t * TOP_K
    p_rows = n + N_EXPERTS * MOE_BLOCK
    nblk = p_rows // MOE_BLOCK
    e_flat = top_e.reshape(-1)
    experts = jnp.arange(N_EXPERTS, dtype=I32)
    counts = jnp.sum((e_flat[:, None] == experts[None, :]).astype(I32), axis=0)
    padded = (counts + MOE_BLOCK - 1) // MOE_BLOCK * MOE_BLOCK
    pad_end = jnp.cumsum(padded)
    cand = jnp.arange(MOE_BLOCK, dtype=I32)[None, :] < (padded - counts)[:, None]
    cand_key = jnp.where(cand, 2 * experts[:, None] + 1, 2 * N_EXPERTS).reshape(-1)
    keys = jnp.concatenate([2 * e_flat, cand_key])
    id_bits = (p_rows - 1).bit_length()
    assert (2 * N_EXPERTS + 1) << id_bits < 2 ** 31
    packed = jnp.left_shift(keys, id_bits) | jnp.arange(p_rows, dtype=I32)
    src = jnp.bitwise_and(jnp.sort(packed), (1 << id_bits) - 1)
    is_pad = src >= n
    pad_rank = jnp.cumsum(is_pad.astype(I32)) - 1
    tok = jnp.right_shift(src, TOP_K.bit_length() - 1)
    gather_tok = jnp.where(is_pad, 0, tok)
    scatter_row = jnp.where(is_pad, n + pad_rank, jnp.bitwise_and(src, TOP_K - 1) * t + tok)
    blk_start = jnp.arange(nblk, dtype=I32) * MOE_BLOCK
    blk_expert = jnp.minimum(jnp.sum((blk_start[:, None] >= pad_end[None, :]).astype(I32), axis=1), N_EXPERTS - 1)
    return gather_tok.reshape(nblk, 1, MOE_BLOCK), scatter_row.reshape(nblk, 1, MOE_BLOCK), blk_expert


MOE_COL_CHUNK = 256


def _moe_body(bexp_ref, gidx_ref, sidx_ref, hn_ref, wgu_ref, bgu_ref, wdn_ref, bdn_ref, y_ref,
              xbuf, ybuf, wgu_bf, wdn_bf, gsem, ssem):
    i = pl.program_id(0)
    last = pl.num_programs(0) - 1
    slot = i % 2

    def gather_copy(tok, s, r):
        return pltpu.make_async_copy(hn_ref.at[tok], xbuf.at[s, pl.ds(r * TOKEN_TILE, TOKEN_TILE)], gsem.at[s])

    def scatter_copy(row, r):
        return pltpu.make_async_copy(ybuf.at[pl.ds(r * TOKEN_TILE, TOKEN_TILE)], y_ref.at[row], ssem)

    @pl.when(i == 0)
    def _():
        for r in range(MOE_BLOCK):
            gather_copy(gidx_ref[0, 0, r], 0, r).start()
        ybuf[...] = jnp.zeros_like(ybuf)

    @pl.when(i >= 0)
    def _():
        for r in range(MOE_BLOCK):
            scatter_copy(sidx_ref[0, 0, r], r).start(priority=r % 2)
            gather_copy(gidx_ref[1, 0, r], 1 - slot, r).start(priority=r % 2)

    @pl.when(jnp.logical_or(i == 0, bexp_ref[i] != bexp_ref[jnp.maximum(i - 1, 0)]))
    def _():
        wgu_bf[...] = wgu_ref[0].astype(BF16)
        wdn_bf[...] = wdn_ref[0].astype(BF16)

    for r in range(MOE_BLOCK):
        gather_copy(0, slot, r).wait()
    x = _get_token_tiles(xbuf.at[slot], MOE_BLOCK).astype(BF16)
    n_col = D_FF // MOE_COL_CHUNK
    acts = []
    for c in range(n_col):
        lo = c * MOE_COL_CHUNK
        g = jnp.dot(x, wgu_bf[:, lo:lo + MOE_COL_CHUNK], preferred_element_type=F32)
        u = jnp.dot(x, wgu_bf[:, D_FF + lo:D_FF + lo + MOE_COL_CHUNK], preferred_element_type=F32)
        g = jnp.minimum(g + bgu_ref[0, :, lo:lo + MOE_COL_CHUNK], SWIGLU_LIMIT)
        u = jnp.clip(u + bgu_ref[0, :, D_FF + lo:D_FF + lo + MOE_COL_CHUNK], -SWIGLU_LIMIT, SWIGLU_LIMIT)
        acts.append(((u + 1.0) * g * jax.nn.sigmoid(SWIGLU_ALPHA * g)).astype(BF16))
    act = jnp.concatenate(acts, axis=-1)
    outs = []
    for c in range(D_MODEL // MOE_COL_CHUNK):
        lo = c * MOE_COL_CHUNK
        outs.append(jnp.dot(act, wdn_bf[:, lo:lo + MOE_COL_CHUNK], preferred_element_type=F32)
                    + bdn_ref[0, :, lo:lo + MOE_COL_CHUNK])
    y = jnp.concatenate(outs, axis=-1)

    for r in range(MOE_BLOCK):
        scatter_copy(0, r).wait()
    _put_token_tiles(ybuf, y)

    @pl.when(i == last)
    def _():
        for r in range(MOE_BLOCK):
            scatter_copy(sidx_ref[1, 0, r], r).start()
        for r in range(MOE_BLOCK):
            scatter_copy(0, r).wait()
        for r in range(MOE_BLOCK):
            gather_copy(0, 1 - slot, r).wait()


def _moe_ffn(hn, plan, w_gu, b_gu, w_dn, b_dn):
    t = hn.shape[0] // TOKEN_TILE
    gather_tok, scatter_row, blk_expert = plan
    nblk = gather_tok.shape[0]
    n_rows = t * TOP_K + (N_EXPERTS + 1) * MOE_BLOCK
    gnext = jnp.concatenate([gather_tok[1:], gather_tok[-1:]], axis=0)
    gpair = jnp.concatenate([gather_tok, gnext], axis=1).reshape(nblk * 2, 1, MOE_BLOCK)
    before_first = (n_rows - MOE_BLOCK + jnp.arange(MOE_BLOCK, dtype=I32)).reshape(1, 1, MOE_BLOCK)
    sprev = jnp.concatenate([before_first, scatter_row[:-1]], axis=0)
    spair = jnp.concatenate([sprev, scatter_row], axis=1).reshape(nblk * 2, 1, MOE_BLOCK)
    wsel = lambda i, be: (be[i], 0, 0)
    pair = lambda i, be: (i, 0, 0)
    grid_spec = pltpu.PrefetchScalarGridSpec(
        num_scalar_prefetch=1,
        grid=(nblk,),
        in_specs=[
            pl.BlockSpec((2, 1, MOE_BLOCK), pair, memory_space=pltpu.SMEM),
            pl.BlockSpec((2, 1, MOE_BLOCK), pair, memory_space=pltpu.SMEM),
            pl.BlockSpec(memory_space=pl.ANY),
            pl.BlockSpec((1, D_MODEL, 2 * D_FF), wsel),
            pl.BlockSpec((1, 1, 2 * D_FF), wsel),
            pl.BlockSpec((1, D_FF, D_MODEL), wsel),
            pl.BlockSpec((1, 1, D_MODEL), wsel),
        ],
        out_specs=pl.BlockSpec(memory_space=pl.ANY),
        scratch_shapes=[
            pltpu.VMEM((2, MOE_BLOCK * TOKEN_TILE, LANES), F32),
            pltpu.VMEM((MOE_BLOCK * TOKEN_TILE, LANES), F32),
            pltpu.VMEM((D_MODEL, 2 * D_FF), BF16),
            pltpu.VMEM((D_FF, D_MODEL), BF16),
            pltpu.SemaphoreType.DMA((2,)),
            pltpu.SemaphoreType.DMA(()),
        ],
    )
    return pl.pallas_call(
        _moe_body,
        grid_spec=grid_spec,
        out_shape=jax.ShapeDtypeStruct((n_rows, TOKEN_TILE, LANES), F32),
        compiler_params=_cparams(("arbitrary",)),
        name="moe_ffn",
    )(blk_expert, gpair, spair, hn.reshape(t, TOKEN_TILE, LANES),
      w_gu.astype(F32), b_gu.astype(F32).reshape(N_EXPERTS, 1, 2 * D_FF),
      w_dn.astype(F32), b_dn.astype(F32).reshape(N_EXPERTS, 1, D_MODEL))


def _combine_body(h_ref, g_ref, *refs):
    y_refs, o_ref = refs[:TOP_K], refs[TOP_K]
    g = g_ref[...]
    m = g.shape[0]
    moe = g[:, 0:1] * _get_token_tiles(y_refs[0], m)
    for k in range(1, TOP_K):
        moe = moe + g[:, k:k + 1] * _get_token_tiles(y_refs[k], m)
    o_ref[...] = h_ref[...] + moe


def _combine(h, y_slots, gates, tm=512):
    t = h.shape[0]
    y2 = y_slots.reshape(-1, LANES)
    row = lambda i: (i, 0)
    kth = lambda k: (lambda i: (k * (t // tm) + i, 0))
    return pl.pallas_call(
        _combine_body,
        grid=(t // tm,),
        in_specs=[pl.BlockSpec((tm, D_MODEL), row), pl.BlockSpec((tm, LANES), row)]
        + [pl.BlockSpec((tm * TOKEN_TILE, LANES), kth(k)) for k in range(TOP_K)],
        out_specs=pl.BlockSpec((tm, D_MODEL), row),
        out_shape=jax.ShapeDtypeStruct((t, D_MODEL), F32),
        compiler_params=_cparams(("parallel",)),
        name="combine",
    )(h, gates, *([y2] * TOP_K))


def _layer(x, norm1_w, w_in, conv_w, a_log, dt_bias, gdn_norm_w, q_norm_w, k_norm_w, rel_bias,
           w_out, norm2_w, w_router, b_router, w_gate_up, b_gate_up, w_down, b_down):
    b, s, d = x.shape
    t = b * s
    x2 = x.reshape(t, d)
    qkv, gate, ba, qb, kb, vb = _in_proj(x2, norm1_w.astype(F32), _pack_w_in(w_in),
                                         q_norm_w.astype(F32), k_norm_w.astype(F32))
    oa = _gdn(qkv.reshape(b, s, CONV_DIM), gate.reshape(b, s, GDN_W), ba.reshape(b, s, LANES),
              conv_w, a_log, dt_bias, gdn_norm_w)
    ob = _dilated_attention(qb, kb, vb, rel_bias, b, s)
    h, hn, top_e, gates = _out_proj(x2, oa.reshape(t, GDN_W), ob, w_out, norm2_w, w_router, b_router)
    plan = _moe_plan(top_e[:, :TOP_K])
    y_slots = _moe_ffn(hn, plan, w_gate_up, b_gate_up, w_down, b_down)
    return _combine(h, y_slots, gates).reshape(b, s, d)


def kernel(x, norm1_w, w_in, conv_w, a_log, dt_bias, gdn_norm_w, q_norm_w, k_norm_w, rel_bias, w_out, norm2_w,
           w_router, b_router, w_gate_up, b_gate_up, w_down, b_down):
    h = x
    for l in range(norm1_w.shape[0]):
        h = _layer(h, norm1_w[l], w_in[l], conv_w[l], a_log[l], dt_bias[l], gdn_norm_w[l], q_norm_w[l],
                   k_norm_w[l], rel_bias, w_out[l], norm2_w[l], w_router[l], b_router[l], w_gate_up[l],
                   b_gate_up[l], w_down[l], b_down[l])
    return h
```
